```python
import jax, jax.numpy as jnp
from jax import lax
import numpy as np

D_MODEL = 1024
BATCH = 8
SEQ = 4096
DEPTH = 2

GRID_W = 64
CTX_LEN = 256
HEAD_DIM = 64
N_FREQ = HEAD_DIM // 4
RET_HEADS = 4
SWA_Q_HEADS = 8
SWA_KV_HEADS = 2
GDN_HEADS = 4
RET_W = RET_HEADS * HEAD_DIM
SWA_W = SWA_Q_HEADS * HEAD_DIM
SWA_KV_W = SWA_KV_HEADS * HEAD_DIM
GDN_W = GDN_HEADS * HEAD_DIM
MIX_W = RET_W + SWA_W + GDN_W
IN_W = 4 * RET_W + SWA_W + 2 * SWA_KV_W + 4 * GDN_W + 4 * GDN_HEADS
WINDOW = 128
Q_BLOCK = 128
RET_CHUNK = 64
GDN_CHUNK = 64
CONV_W = 3
D_FF = ((8 * D_MODEL + 3 * 256 - 1) // (3 * 256)) * 256
ROPE_BASE = 10000.0
EPS = 1e-6
NEG_INF = -1e30
N_MOD = 6

kernel_name = 'hybrid_parallel_heads_dit_block'


def rmsnorm(x, w):
    xf = x.astype(jnp.float32)
    y = xf * lax.rsqrt(jnp.mean(xf * xf, axis=-1, keepdims=True) + EPS)
    return (y * w.astype(jnp.float32)).astype(x.dtype)


def head_rms(o):
    return o * lax.rsqrt(jnp.mean(o * o, axis=-1, keepdims=True) + EPS)


def l2norm(t):
    return t * lax.rsqrt(jnp.sum(t * t, axis=-1, keepdims=True) + EPS)


def flip(t):
    return jnp.flip(t, axis=2)


def heads(t, h):
    b, l, _ = t.shape
    return t.reshape(b, l, h, HEAD_DIM).transpose(0, 2, 1, 3)


def merge(t):
    b, h, l, d = t.shape
    return t.transpose(0, 2, 1, 3).reshape(b, l, h * d)


def axial_rope(rows):
    row = jnp.repeat(jnp.arange(rows), GRID_W).astype(jnp.float32)
    col = jnp.tile(jnp.arange(GRID_W), rows).astype(jnp.float32)
    freqs = ROPE_BASE ** (-jnp.arange(N_FREQ, dtype=jnp.float32) / N_FREQ)
    ang = jnp.stack([row[:, None] * freqs, col[:, None] * freqs], axis=1)
    return jnp.cos(ang), jnp.sin(ang)


def apply_rope(t, cos, sin):
    tr = t.reshape(*t.shape[:-1], 2, 2, N_FREQ).astype(jnp.float32)
    t1, t2 = tr[..., 0, :], tr[..., 1, :]
    out = jnp.stack([t1 * cos - t2 * sin, t1 * sin + t2 * cos], axis=-2)
    return out.reshape(t.shape).astype(t.dtype)


def split_proj(p):
    sizes = (RET_W, RET_W, RET_W, RET_W,
             SWA_W, SWA_KV_W, SWA_KV_W,
             GDN_W, GDN_W, GDN_W, GDN_W, 2 * GDN_HEADS, 2 * GDN_HEADS)
    out, start = [], 0
    for s in sizes:
        out.append(p[..., start:start + s])
        start += s
    return out


def short_conv(t, w):
    l = t.shape[1]
    pad = CONV_W // 2
    tp = jnp.pad(t, ((0, 0), (pad, pad), (0, 0)))
    return sum(tp[:, j:j + l] * w[j] for j in range(CONV_W))


def retention_chunked(q, k, v, log_gamma, s0):
    b, h, l, dk = q.shape
    dv = v.shape[-1]
    n = l // RET_CHUNK

    def chunks(t):
        return t.reshape(b, h, n, RET_CHUNK, t.shape[-1]).transpose(2, 0, 1, 3, 4)

    qn, kn, vn = chunks(q), chunks(k), chunks(v)
    idx = jnp.arange(RET_CHUNK, dtype=jnp.float32)
    lg = log_gamma.astype(jnp.float32)[:, None]
    diff = idx[:, None] - idx[None, :]
    dmask = jnp.where(diff >= 0, jnp.exp(lg[:, :, None] * jnp.maximum(diff, 0.0)), 0.0)
    q_dec = jnp.exp(lg * (idx + 1.0))[:, :, None]
    k_dec = jnp.exp(lg * (RET_CHUNK - 1.0 - idx))[:, :, None]
    c_dec = jnp.exp(log_gamma.astype(jnp.float32) * RET_CHUNK)[:, None, None]
    intra = jnp.einsum('nbhij,nbhje->nbhie',
                       jnp.einsum('nbhid,nbhjd->nbhij', qn, kn) * dmask, vn)

    def step(s, inp):
        qd, kd, vc = inp
        cross = jnp.einsum('bhid,bhde->bhie', qd, s)
        s = s * c_dec + jnp.einsum('bhjd,bhje->bhde', kd, vc)
        return s, cross

    s_fin, cross = lax.scan(step, s0, (qn * q_dec, kn * k_dec, vn))
    o = (intra + cross).transpose(1, 2, 0, 3, 4).reshape(b, h, l, dv)
    return o, s_fin


def gdn_chunked(q, k, v, g, beta, s0):
    b, h, l, dk = q.shape
    dv = v.shape[-1]
    n = l // GDN_CHUNK

    def chunks(t):
        return t.reshape(b, h, n, GDN_CHUNK, t.shape[-1]).transpose(2, 0, 1, 3, 4)

    qn = chunks(q) * dk ** -0.5
    kn, vn = chunks(k), chunks(v)
    gn = g.reshape(b, h, n, GDN_CHUNK).transpose(2, 0, 1, 3)
    bn = beta.reshape(b, h, n, GDN_CHUNK).transpose(2, 0, 1, 3)
    gc = jnp.cumsum(gn, axis=-1)
    idx = jnp.arange(GDN_CHUNK)
    tri = idx[:, None] >= idx[None, :]
    strict = idx[:, None] > idx[None, :]
    decay = jnp.exp(jnp.where(tri, gc[..., :, None] - gc[..., None, :], -jnp.inf))
    kk = jnp.einsum('nbhid,nbhjd->nbhij', kn, kn)
    m = jnp.where(strict, kk * bn[..., :, None] * decay, 0.0)
    t_mat = m + jnp.eye(GDN_CHUNK, dtype=jnp.float32)
    rhs = jnp.concatenate([vn * bn[..., None], kn * (bn * jnp.exp(gc))[..., None]], axis=-1)
    sol = lax.linalg.triangular_solve(t_mat, rhs, left_side=True, lower=True,
                                      unit_diagonal=True)
    u, w = sol[..., :dv], sol[..., dv:]
    attn = jnp.einsum('nbhid,nbhjd->nbhij', qn, kn) * decay
    q_dec = qn * jnp.exp(gc)[..., None]
    k_tail = kn * jnp.exp(gc[..., -1:] - gc)[..., None]
    g_tot = jnp.exp(gc[..., -1])[..., None, None]

    def step(s, inp):
        u_i, w_i, a_i, qd_i, kt_i, gt_i = inp
        v_new = u_i - jnp.einsum('bhid,bhde->bhie', w_i, s)
        o_i = jnp.einsum('bhid,bhde->bhie', qd_i, s) + jnp.einsum('bhij,bhje->bhie', a_i, v_new)
        s = s * gt_i + jnp.einsum('bhjd,bhje->bhde', kt_i, v_new)
        return s, o_i

    s_fin, o = lax.scan(step, s0, (u, w, attn, q_dec, k_tail, g_tot))
    o = o.transpose(1, 2, 0, 3, 4).reshape(b, h, l, dv)
    return o, s_fin


def swa_latent(q, k, v, k_ctx, v_ctx, sinks):
    b, hq, l, d = q.shape
    g = hq // SWA_KV_HEADS
    lc = k_ctx.shape[2]
    nblk = l // Q_BLOCK
    span = Q_BLOCK + 2 * WINDOW
    qg = q.reshape(b, SWA_KV_HEADS, g, l, d)
    kp = jnp.pad(k, ((0, 0), (0, 0), (WINDOW, WINDOW), (0, 0)))
    vp = jnp.pad(v, ((0, 0), (0, 0), (WINDOW, WINDOW), (0, 0)))
    scale = d ** -0.5
    sink = jnp.broadcast_to(sinks.astype(jnp.float32).reshape(1, SWA_KV_HEADS, g, 1, 1),
                            (b, SWA_KV_HEADS, g, Q_BLOCK, 1))

    def block(i):
        start = i * Q_BLOCK
        qb = lax.dynamic_slice_in_dim(qg, start, Q_BLOCK, axis=3)
        kb = lax.dynamic_slice_in_dim(kp, start, span, axis=2)
        vb = lax.dynamic_slice_in_dim(vp, start, span, axis=2)
        qpos = start + jnp.arange(Q_BLOCK)
        kpos = start - WINDOW + jnp.arange(span)
        valid = ((jnp.abs(qpos[:, None] - kpos[None, :]) <= WINDOW)
                 & (kpos >= 0)[None, :] & (kpos < l)[None, :])
        s_loc = jnp.einsum('bkgqd,bkjd->bkgqj', qb, kb).astype(jnp.float32) * scale
        s_loc = jnp.where(valid, s_loc, NEG_INF)
        s_ctx = jnp.einsum('bkgqd,bkjd->bkgqj', qb, k_ctx).astype(jnp.float32) * scale
        p = jax.nn.softmax(jnp.concatenate([s_loc, s_ctx, sink], axis=-1), axis=-1)
        p_loc = p[..., :span].astype(v.dtype)
        p_ctx = p[..., span:span + lc].astype(v.dtype)
        return (jnp.einsum('bkgqj,bkjd->bkgqd', p_loc, vb)
                + jnp.einsum('bkgqj,bkjd->bkgqd', p_ctx, v_ctx))

    out = lax.map(block, jnp.arange(nblk))
    return out.transpose(1, 2, 3, 0, 4, 5).reshape(b, hq, l, d)


def swa_context(q, k, v, sinks):
    b, hq, lc, d = q.shape
    g = hq // SWA_KV_HEADS
    qg = q.reshape(b, SWA_KV_HEADS, g, lc, d)
    s = jnp.einsum('bkgqd,bkjd->bkgqj', qg, k).astype(jnp.float32) * d ** -0.5
    sink = jnp.broadcast_to(sinks.astype(jnp.float32).reshape(1, SWA_KV_HEADS, g, 1, 1),
                            (b, SWA_KV_HEADS, g, lc, 1))
    p = jax.nn.softmax(jnp.concatenate([s, sink], axis=-1), axis=-1)[..., :lc].astype(v.dtype)
    return jnp.einsum('bkgqj,bkjd->bkgqd', p, v).reshape(b, hq, lc, d)


def retention_group(pc, pl, cos, sin, ret_rate, norm_w, need_ctx):
    log_gamma = jnp.log1p(-jnp.exp2(-ret_rate.astype(jnp.float32)))

    def prep(q, k, v, rope):
        q, k, v = (heads(t.astype(jnp.float32), RET_HEADS) for t in (q, k, v))
        if rope:
            q, k = apply_rope(q, cos, sin), apply_rope(k, cos, sin)
        return q, k * HEAD_DIM ** -0.5, v

    qc, kc, vc = prep(pc[0], pc[1], pc[2], False)
    ql, kl, vl = prep(pl[0], pl[1], pl[2], True)
    s0 = jnp.zeros((qc.shape[0], RET_HEADS, HEAD_DIM, HEAD_DIM), jnp.float32)
    oc_f, s_f = retention_chunked(qc, kc, vc, log_gamma[0], s0)
    oc_b, s_b = retention_chunked(flip(qc), flip(kc), flip(vc), log_gamma[1], s0)
    ol_f, _ = retention_chunked(ql, kl, vl, log_gamma[0], s_f)
    ol_b, _ = retention_chunked(flip(ql), flip(kl), flip(vl), log_gamma[1], s_b)

    def out(o, gate):
        y = merge(head_rms(o)) * norm_w.astype(jnp.float32) * jax.nn.silu(gate.astype(jnp.float32))
        return y.astype(gate.dtype)

    y_l = out(ol_f + flip(ol_b), pl[3])
    y_c = out(oc_f + flip(oc_b), pc[3]) if need_ctx else None
    return y_l, y_c


def swa_group(pc, pl, cos, sin, sinks, need_ctx):
    qc, kc, vc = heads(pc[0], SWA_Q_HEADS), heads(pc[1], SWA_KV_HEADS), heads(pc[2], SWA_KV_HEADS)
    ql = apply_rope(heads(pl[0], SWA_Q_HEADS), cos, sin)
    kl = apply_rope(heads(pl[1], SWA_KV_HEADS), cos, sin)
    vl = heads(pl[2], SWA_KV_HEADS)
    y_l = merge(swa_latent(ql, kl, vl, kc, vc, sinks))
    y_c = merge(swa_context(qc, kc, vc, sinks)) if need_ctx else None
    return y_l, y_c


def gdn_group(pc, pl, conv_w, a_log, dt_bias, norm_w, need_ctx):
    a_rate = jnp.exp(a_log.astype(jnp.float32))
    dtb = dt_bias.astype(jnp.float32)

    def prep(q, k, v, a, bg):
        b, l, _ = q.shape
        qkv = jax.nn.silu(short_conv(jnp.concatenate([q, k, v], axis=-1).astype(jnp.float32),
                                     conv_w.astype(jnp.float32)))
        q, k, v = (heads(qkv[..., i * GDN_W:(i + 1) * GDN_W], GDN_HEADS) for i in range(3))
        a = a.astype(jnp.float32).reshape(b, l, 2, GDN_HEADS)
        bg = bg.astype(jnp.float32).reshape(b, l, 2, GDN_HEADS)
        g = (-a_rate * jax.nn.softplus(a + dtb)).transpose(2, 0, 3, 1)
        beta = jax.nn.sigmoid(bg).transpose(2, 0, 3, 1)
        return l2norm(q), l2norm(k), v, g, beta

    qc, kc, vc, gcx, bc = prep(pc[0], pc[1], pc[2], pc[4], pc[5])
    ql, kl, vl, glt, bl = prep(pl[0], pl[1], pl[2], pl[4], pl[5])
    s0 = jnp.zeros((qc.shape[0], GDN_HEADS, HEAD_DIM, HEAD_DIM), jnp.float32)
    oc_f, s_f = gdn_chunked(qc, kc, vc, gcx[0], bc[0], s0)
    oc_b, s_b = gdn_chunked(flip(qc), flip(kc), flip(vc), flip(gcx[1]), flip(bc[1]), s0)
    ol_f, _ = gdn_chunked(ql, kl, vl, glt[0], bl[0], s_f)
    ol_b, _ = gdn_chunked(flip(ql), flip(kl), flip(vl), flip(glt[1]), flip(bl[1]), s_b)

    def out(o, z):
        y = merge(head_rms(o) * norm_w.astype(jnp.float32)) * jax.nn.silu(z.astype(jnp.float32))
        return y.astype(z.dtype)

    y_l = out(ol_f + flip(ol_b), pl[3])
    y_c = out(oc_f + flip(oc_b), pc[3]) if need_ctx else None
    return y_l, y_c


def mixer(hc, hl, cos, sin, w_in, ret_rate, ret_norm_w, sinks, conv_w, a_log, dt_bias,
          gdn_norm_w, w_out, need_ctx):
    pl = split_proj(hl @ w_in)
    pc = split_proj(hc @ w_in)
    r_l, r_c = retention_group(pc[0:4], pl[0:4], cos, sin, ret_rate, ret_norm_w, need_ctx)
    s_l, s_c = swa_group(pc[4:7], pl[4:7], cos, sin, sinks, need_ctx)
    g_l, g_c = gdn_group(pc[7:13], pl[7:13], conv_w, a_log, dt_bias, gdn_norm_w, need_ctx)
    y_l = jnp.concatenate([r_l, s_l, g_l], axis=-1) @ w_out
    y_c = jnp.concatenate([r_c, s_c, g_c], axis=-1) @ w_out if need_ctx else None
    return y_l, y_c


def swiglu(h, w1, w2):
    gu = h @ w1
    return (jax.nn.silu(gu[..., :D_FF]) * gu[..., D_FF:]) @ w2


def setup_inputs(seed: int = 0) -> dict:
    key = jax.random.key(seed)
    ks = jax.random.split(key, 20)

    def nrm(k, shape, scale):
        return jax.random.normal(k, shape, jnp.float32) * scale

    dt = jnp.exp(jax.random.uniform(ks[15], (DEPTH, 2, GDN_HEADS), jnp.float32,
                                    minval=np.log(1e-3), maxval=np.log(1e-1)))
    return {
        'x': nrm(ks[0], (BATCH, SEQ, D_MODEL), 1.0),
        'c': nrm(ks[1], (BATCH, D_MODEL), 1.0),
        'ctx': nrm(ks[2], (BATCH, CTX_LEN, D_MODEL), 1.0),
        'c_ctx': nrm(ks[3], (D_MODEL,), 1.0),
        'ada_w': nrm(ks[4], (DEPTH, D_MODEL, N_MOD * D_MODEL), 0.5 * D_MODEL ** -0.5),
        'ada_b': nrm(ks[5], (DEPTH, N_MOD * D_MODEL), 0.02),
        'norm_mix_w': 1.0 + nrm(ks[6], (DEPTH, D_MODEL), 0.02),
        'norm_ffn_w': 1.0 + nrm(ks[7], (DEPTH, D_MODEL), 0.02),
        'w_in': nrm(ks[8], (DEPTH, D_MODEL, IN_W), D_MODEL ** -0.5),
        'ret_rate': 5.0 + jnp.arange(RET_HEADS, dtype=jnp.float32) + nrm(ks[9], (DEPTH, 2, RET_HEADS), 0.1),
        'ret_norm_w': 1.0 + nrm(ks[10], (DEPTH, RET_W), 0.02),
        'swa_sinks': nrm(ks[11], (DEPTH, SWA_Q_HEADS), 1.0),
        'gdn_conv_w': nrm(ks[12], (DEPTH, CONV_W, 3 * GDN_W), CONV_W ** -0.5),
        'gdn_a_log': jnp.log(jax.random.uniform(ks[13], (DEPTH, 2, GDN_HEADS), jnp.float32,
                                                minval=1.0, maxval=16.0)),
        'gdn_dt_bias': dt + jnp.log(-jnp.expm1(-dt)),
        'gdn_norm_w': 1.0 + nrm(ks[14], (DEPTH, HEAD_DIM), 0.02),
        'w_out': nrm(ks[16], (DEPTH, MIX_W, D_MODEL), MIX_W ** -0.5),
        'w_ffn_in': nrm(ks[17], (DEPTH, D_MODEL, 2 * D_FF), D_MODEL ** -0.5),
        'w_ffn_out': nrm(ks[18], (DEPTH, D_FF, D_MODEL), D_FF ** -0.5),
        'final_norm_w': 1.0 + nrm(ks[19], (D_MODEL,), 0.02),
    }


def reference(x, c, ctx, c_ctx, ada_w, ada_b, norm_mix_w, norm_ffn_w, w_in, ret_rate,
              ret_norm_w, swa_sinks, gdn_conv_w, gdn_a_log, gdn_dt_bias, gdn_norm_w, w_out,
              w_ffn_in, w_ffn_out, final_norm_w):
    l = x.shape[1]
    rows = l // GRID_W
    cos, sin = axial_rope(rows)
    c_s = jax.nn.silu(c)
    cc_s = jax.nn.silu(c_ctx)
    xc = ctx
    for layer in range(DEPTH):
        need_ctx = layer < DEPTH - 1
        mod = (c_s @ ada_w[layer] + ada_b[layer])[:, None, :]
        sh1, sc1, g1, sh2, sc2, g2 = jnp.split(mod, N_MOD, axis=-1)
        modc = cc_s @ ada_w[layer] + ada_b[layer]
        shc1, scc1, gc1, shc2, scc2, gc2 = jnp.split(modc, N_MOD, axis=-1)
        hl = rmsnorm(x, norm_mix_w[layer]) * (1.0 + sc1) + sh1
        hc = rmsnorm(xc, norm_mix_w[layer]) * (1.0 + scc1) + shc1
        y_l, y_c = mixer(hc, hl, cos, sin, w_in[layer], ret_rate[layer], ret_norm_w[layer],
                         swa_sinks[layer], gdn_conv_w[layer], gdn_a_log[layer],
                         gdn_dt_bias[layer], gdn_norm_w[layer], w_out[layer], need_ctx)
        x = x + g1 * y_l
        x = x + g2 * swiglu(rmsnorm(x, norm_ffn_w[layer]) * (1.0 + sc2) + sh2,
                            w_ffn_in[layer], w_ffn_out[layer])
        if need_ctx:
            xc = xc + gc1 * y_c
            xc = xc + gc2 * swiglu(rmsnorm(xc, norm_ffn_w[layer]) * (1.0 + scc2) + shc2,
                                   w_ffn_in[layer], w_ffn_out[layer])
    return rmsnorm(x, final_norm_w)
```

```python
import functools

import numpy as np
import jax
import jax.numpy as jnp
from jax import lax
from jax.experimental import pallas as pl
from jax.experimental.pallas import tpu as pltpu

F32 = jnp.float32
BF16 = jnp.bfloat16

HEAD_DIM = 64
N_FREQ = HEAD_DIM // 4
GRID_W = 64
RET_HEADS = 4
SWA_Q_HEADS = 8
SWA_KV_HEADS = 2
GDN_HEADS = 4
RET_W = RET_HEADS * HEAD_DIM
SWA_W = SWA_Q_HEADS * HEAD_DIM
SWA_KV_W = SWA_KV_HEADS * HEAD_DIM
GDN_W = GDN_HEADS * HEAD_DIM
WINDOW = 128
Q_BLOCK = 128
GDN_CHUNK = 64
CONV_W = 3
ROPE_BASE = 10000.0
EPS = 1e-6
NEG_INF = -1e30
N_MOD = 6

LANES = 128
TILE = 256
MOD_ROWS = 16
VMEM_LIMIT = 56 * 1024 * 1024

COL_RET = 0
COL_SWA_Q = 4 * RET_W
COL_SWA_KV = COL_SWA_Q + SWA_W
COL_GDN_QKV = COL_SWA_KV + 2 * SWA_KV_W
COL_GDN_Z = COL_GDN_QKV + 3 * GDN_W
COL_AB = COL_GDN_Z + GDN_W
IN_W = COL_AB + 4 * GDN_HEADS
IN_W_PAD = COL_AB + LANES


def _dot(a, b):
    return jnp.dot(a, b, preferred_element_type=F32)


def _dot_nt(a, b):
    return lax.dot_general(a, b, (((1,), (1,)), ((), ())), preferred_element_type=F32)


def _split3(x):
    hi = x.astype(BF16)
    r = x - hi.astype(F32)
    mid = r.astype(BF16)
    lo = (r - mid.astype(F32)).astype(BF16)
    return hi, mid, lo


def _dot_x_sel(x, sel):
    hi, mid, lo = _split3(x)
    return _dot(hi, sel) + _dot(mid, sel) + _dot(lo, sel)


def _dot_sel_x(sel, x):
    hi, mid, lo = _split3(x)
    return _dot(sel, hi) + _dot(sel, mid) + _dot(sel, lo)


def _silu(x):
    return x * jax.nn.sigmoid(x)


def _head_masks(rows, width):
    lane = lax.broadcasted_iota(jnp.int32, (rows, width), 1)
    return [(lane // HEAD_DIM) == h for h in range(width // HEAD_DIM)]


def _head_sum(x, ones_bd):
    return _dot_x_sel(x, ones_bd)


def _cparams(n_axes):
    return pltpu.CompilerParams(dimension_semantics=("arbitrary",) * n_axes,
                                vmem_limit_bytes=VMEM_LIMIT)


def _ada_kernel(c_ref, w_ref, b_ref, o_ref):
    cs = _silu(c_ref[...])
    o_ref[0] = _dot(cs.astype(BF16), w_ref[0].astype(BF16)) + b_ref[0]


def _ada_call(c_pad, ada_w, ada_b):
    depth, d, n = ada_w.shape
    tn = 1536
    return pl.pallas_call(
        _ada_kernel,
        grid=(depth, n // tn),
        in_specs=[pl.BlockSpec((MOD_ROWS, d), lambda l, j: (0, 0)),
                  pl.BlockSpec((1, d, tn), lambda l, j: (l, 0, j)),
                  pl.BlockSpec((1, 1, tn), lambda l, j: (l, 0, j))],
        out_specs=pl.BlockSpec((1, MOD_ROWS, tn), lambda l, j: (l, 0, j)),
        out_shape=jax.ShapeDtypeStruct((depth, MOD_ROWS, n), F32),
        compiler_params=_cparams(2),
        name="ada_mod",
    )(c_pad, ada_w, ada_b.reshape(depth, 1, n))


def _rope(t, cos, sin_signed):
    lane = lax.broadcasted_iota(jnp.int32, t.shape, 1)
    first_half = (lane % (2 * N_FREQ)) < N_FREQ
    partner = jnp.where(first_half, pltpu.roll(t, LANES - N_FREQ, 1), pltpu.roll(t, N_FREQ, 1))
    return t * cos + partner * sin_signed


def _inproj_kernel(x_ref, sh_ref, sc_ref, nw_ref, w_ref, cos_ref, sin_ref,
                   ret_ref, swq_ref, swkv_ref, gqkv_ref, gz_ref, ab_ref):
    x = x_ref[0]
    y = x * lax.rsqrt(jnp.mean(x * x, axis=-1, keepdims=True) + EPS) * nw_ref[...]
    h = (y * (1.0 + sc_ref[0]) + sh_ref[0]).astype(BF16)
    cos = cos_ref[...]
    sin = sin_ref[...]

    def proj(col, width, out_ref, out_col, roped_lanes):
        r = _dot(h, w_ref[:, col:col + width])
        for g in range(width // LANES):
            rg = r[:, g * LANES:(g + 1) * LANES]
            if g * LANES < roped_lanes:
                rg = _rope(rg, cos, sin)
            out_ref[0, :, out_col + g * LANES:out_col + (g + 1) * LANES] = rg

    for g in range(4):
        proj(COL_RET + g * RET_W, RET_W, ret_ref, g * RET_W, RET_W if g < 2 else 0)
    for g in range(SWA_W // TILE):
        proj(COL_SWA_Q + g * TILE, TILE, swq_ref, g * TILE, TILE)
    proj(COL_SWA_KV, 2 * SWA_KV_W, swkv_ref, 0, SWA_KV_W)
    for g in range(3):
        proj(COL_GDN_QKV + g * GDN_W, GDN_W, gqkv_ref, g * GDN_W, 0)
    proj(COL_GDN_Z, GDN_W, gz_ref, 0, 0)
    proj(COL_AB, LANES, ab_ref, 0, 0)


def _inproj_call(xcat, mod, norm_w, w_in_bf, cos_t, sin_t, n_lat_tiles, ctx_row):
    b, tt, d = xcat.shape
    nt = tt // TILE

    def mod_row(bi, t):
        return jnp.where(t >= n_lat_tiles, ctx_row, bi)

    widths = (4 * RET_W, SWA_W, 2 * SWA_KV_W, 3 * GDN_W, GDN_W, LANES)
    return pl.pallas_call(
        _inproj_kernel,
        grid=(b, nt),
        in_specs=[pl.BlockSpec((1, TILE, d), lambda bi, t: (bi, t, 0)),
                  pl.BlockSpec((1, 1, d), lambda bi, t: (mod_row(bi, t), 0, 0)),
                  pl.BlockSpec((1, 1, d), lambda bi, t: (mod_row(bi, t), 0, 1)),
                  pl.BlockSpec((1, d), lambda bi, t: (0, 0)),
                  pl.BlockSpec((d, IN_W_PAD), lambda bi, t: (0, 0)),
                  pl.BlockSpec((TILE, LANES), lambda bi, t: (t, 0)),
                  pl.BlockSpec((TILE, LANES), lambda bi, t: (t, 0))],
        out_specs=[pl.BlockSpec((1, TILE, w), lambda bi, t: (bi, t, 0)) for w in widths],
        out_shape=[jax.ShapeDtypeStruct((b, tt, w), F32) for w in widths],
        compiler_params=_cparams(2),
        name="in_proj",
    )(xcat, mod, mod, norm_w.reshape(1, d), w_in_bf, cos_t, sin_t)


def _chunk_of(p, c, n_lat):
    return jnp.where(c == 0, n_lat, jnp.where(p == 0, c - 1, n_lat - c))


def _ret_kernel(n_lat, lg_ref, lge_ref, nw_ref, ones_ref, r_ref, y_ref, s_ref, o_ref):
    p = pl.program_id(1)
    c = pl.program_id(2)
    chunk = _chunk_of(p, c, n_lat)
    rows = pl.ds(pl.multiple_of(chunk * TILE, TILE), TILE)

    @pl.when(c == 0)
    def _():
        s_ref[...] = jnp.zeros_like(s_ref)

    blk = r_ref[0]
    q = blk[:, 0:RET_W] * (HEAD_DIM ** -0.5)
    k = blk[:, RET_W:2 * RET_W]
    v = blk[:, 2 * RET_W:3 * RET_W]
    masks = _head_masks(TILE, RET_W)
    ri = lax.broadcasted_iota(jnp.int32, (TILE, TILE), 0)
    ci = lax.broadcasted_iota(jnp.int32, (TILE, TILE), 1)
    bd = (ri // HEAD_DIM) == (ci // HEAD_DIM)
    idx = ri.astype(F32)

    def sweep(direction):
        lge = lge_ref[direction:direction + 1, :]
        if direction == 0:
            q_dec = jnp.exp(lge * (idx + 1.0))
            k_dec = jnp.exp(lge * (TILE - 1.0 - idx))
        else:
            q_dec = jnp.exp(lge * (TILE - idx))
            k_dec = jnp.exp(lge * idx)
        c_dec = jnp.exp(lge * float(TILE))
        s = s_ref[...]
        cross = _dot((q * q_dec).astype(BF16), s.astype(BF16))
        kt = (k * k_dec).T
        upd = _dot(kt.astype(BF16), v.astype(BF16))
        s_ref[...] = s * c_dec + jnp.where(bd, upd, 0.0)
        return cross

    @pl.when(p == 0)
    def _():
        diff = (ri - ci).astype(F32)
        kb = k.astype(BF16)
        o = sweep(0)
        for h in range(RET_HEADS):
            lgf = lg_ref[0, h]
            lgb = lg_ref[1, h]
            dm = (jnp.where(diff >= 0, jnp.exp(lgf * jnp.maximum(diff, 0.0)), 0.0)
                  + jnp.where(diff <= 0, jnp.exp(lgb * jnp.maximum(-diff, 0.0)), 0.0))
            sc = _dot_nt(jnp.where(masks[h], q, 0.0).astype(BF16), kb)
            o = o + _dot((sc * dm).astype(BF16), jnp.where(masks[h], v, 0.0).astype(BF16))
        o_ref[rows, :] = o

    @pl.when(p == 1)
    def _():
        o = o_ref[rows, :] + sweep(1)
        ms = _head_sum(o * o, ones_ref[...]) * (1.0 / HEAD_DIM)
        gate = blk[:, 3 * RET_W:4 * RET_W]
        y_ref[0] = o * lax.rsqrt(ms + EPS) * nw_ref[...] * _silu(gate)


def _ret_call(ret, log_gamma, ret_norm_w, ones_bd, n_lat):
    b, tt, _ = ret.shape
    nt = tt // TILE
    lge = jnp.repeat(log_gamma, HEAD_DIM, axis=-1)

    def in_idx(bi, p, c):
        return (bi, _chunk_of(p, c, n_lat), 0)

    def out_idx(bi, p, c):
        return (bi, jnp.where(p == 0, n_lat, _chunk_of(p, c, n_lat)), 0)

    return pl.pallas_call(
        functools.partial(_ret_kernel, n_lat),
        grid=(b, 2, nt),
        in_specs=[pl.BlockSpec(memory_space=pltpu.SMEM),
                  pl.BlockSpec((2, RET_W), lambda bi, p, c: (0, 0)),
                  pl.BlockSpec((1, RET_W), lambda bi, p, c: (0, 0)),
                  pl.BlockSpec((RET_W, RET_W), lambda bi, p, c: (0, 0)),
                  pl.BlockSpec((1, TILE, 4 * RET_W), in_idx)],
        out_specs=pl.BlockSpec((1, TILE, RET_W), out_idx),
        out_shape=jax.ShapeDtypeStruct((b, tt, RET_W), F32),
        scratch_shapes=[pltpu.VMEM((RET_W, RET_W), F32), pltpu.VMEM((tt, RET_W), F32)],
        compiler_params=_cparams(3),
        name="retention",
    )(log_gamma, lge, ret_norm_w.reshape(1, RET_W), ones_bd, ret)


def _swa_kernel(n_lat_blk, sink_ref, q_ref, kv_ref, y_ref):
    j = pl.program_id(1)
    is_lat = j < n_lat_blk
    jc = jnp.minimum(j, n_lat_blk - 1)
    jp = jnp.maximum(jc - 1, 0)
    jn = jnp.minimum(jc + 1, n_lat_blk - 1)
    lat_rows = n_lat_blk * Q_BLOCK

    def piece(blk):
        return kv_ref[0, pl.ds(pl.multiple_of(blk * Q_BLOCK, Q_BLOCK), Q_BLOCK), :]

    kv_all = jnp.concatenate([piece(jp), piece(jc), piece(jn), kv_ref[0, lat_rows:, :]], axis=0)
    k_all = kv_all[:, 0:SWA_KV_W]
    v_all = kv_all[:, SWA_KV_W:2 * SWA_KV_W]
    n_keys = k_all.shape[0]
    k_var = (k_all.astype(BF16), pltpu.roll(k_all, HEAD_DIM, 1).astype(BF16))
    v_var = (v_all.astype(BF16), pltpu.roll(v_all, HEAD_DIM, 1).astype(BF16))

    ql = lax.broadcasted_iota(jnp.int32, (Q_BLOCK, n_keys), 0)
    kc = lax.broadcasted_iota(jnp.int32, (Q_BLOCK, n_keys), 1)
    kl = kc % Q_BLOCK
    pc = kc // Q_BLOCK
    valid = ((pc >= 3)
             | (is_lat & (pc == 1))
             | (is_lat & (j >= 1) & (pc == 0) & (kl >= ql))
             | (is_lat & (j <= n_lat_blk - 2) & (pc == 2) & (kl <= ql)))
    lane = lax.broadcasted_iota(jnp.int32, (Q_BLOCK, LANES), 1)
    low = lane < HEAD_DIM

    for t in range(SWA_W // LANES):
        qt = q_ref[0, :, t * LANES:(t + 1) * LANES] * (HEAD_DIM ** -0.5)
        halves = []
        for e in range(2):
            head = 2 * t + e
            grp = head // (SWA_Q_HEADS // SWA_KV_HEADS)
            var = 0 if grp == e else 1
            qm = jnp.where(low if e == 0 else ~low, qt, 0.0).astype(BF16)
            s = jnp.where(valid, _dot_nt(qm, k_var[var]), NEG_INF)
            sink = sink_ref[0, head]
            m = jnp.maximum(jnp.max(s, axis=-1, keepdims=True), sink)
            pr = jnp.exp(s - m)
            den = jnp.sum(pr, axis=-1, keepdims=True) + jnp.exp(sink - m)
            halves.append(_dot(pr.astype(BF16), v_var[var]) / den)
        y_ref[0, :, t * LANES:(t + 1) * LANES] = jnp.where(low, halves[0], halves[1])


def _swa_call(swq, swkv, sinks, n_lat_blk, n_q_blk):
    b, tt, _ = swq.shape
    return pl.pallas_call(
        functools.partial(_swa_kernel, n_lat_blk),
        grid=(b, n_q_blk),
        in_specs=[pl.BlockSpec(memory_space=pltpu.SMEM),
                  pl.BlockSpec((1, Q_BLOCK, SWA_W), lambda bi, j: (bi, j, 0)),
                  pl.BlockSpec((1, tt, 2 * SWA_KV_W), lambda bi, j: (bi, 0, 0))],
        out_specs=pl.BlockSpec((1, Q_BLOCK, SWA_W), lambda bi, j: (bi, j, 0)),
        out_shape=jax.ShapeDtypeStruct((b, tt, SWA_W), F32),
        compiler_params=_cparams(2),
        name="swa",
    )(sinks.reshape(1, SWA_Q_HEADS), swq, swkv)


def _gdn_prep_kernel(n_lat, x_ref, hp_ref, hn_ref, ab_ref, cw_ref, alog_ref, dtb_ref, ones_ref,
                     qkv_ref, gb_ref):
    t = pl.program_id(1)
    x = x_ref[0]
    has_prev = ((t > 0) & (t < n_lat)).astype(F32)
    has_next = (t < n_lat - 1).astype(F32)
    row = lax.broadcasted_iota(jnp.int32, x.shape, 0)
    prev_row = hp_ref[0, 7:8, :] * has_prev
    next_row = hn_ref[0, 0:1, :] * has_next
    xm1 = jnp.where(row == 0, prev_row, pltpu.roll(x, 1, 0))
    xp1 = jnp.where(row == TILE - 1, next_row, pltpu.roll(x, TILE - 1, 0))
    conv = xm1 * cw_ref[0:1, :] + x * cw_ref[1:2, :] + xp1 * cw_ref[2:3, :]
    s = _silu(conv)
    ones_bd = ones_ref[...]

    def l2n(u):
        return u * lax.rsqrt(_head_sum(u * u, ones_bd) + EPS)

    qkv_ref[0, :, 0:GDN_W] = l2n(s[:, 0:GDN_W]) * (HEAD_DIM ** -0.5)
    qkv_ref[0, :, GDN_W:2 * GDN_W] = l2n(s[:, GDN_W:2 * GDN_W])
    qkv_ref[0, :, 2 * GDN_W:3 * GDN_W] = s[:, 2 * GDN_W:3 * GDN_W]

    ab = ab_ref[0]
    lane = lax.broadcasted_iota(jnp.int32, ab.shape, 1)
    g = -jnp.exp(alog_ref[...]) * jax.nn.softplus(ab + dtb_ref[...])
    beta = jax.nn.sigmoid(ab)
    gb_ref[0] = jnp.where(lane < 2 * GDN_HEADS, g, jnp.where(lane < 4 * GDN_HEADS, beta, 0.0))


def _gdn_prep_call(gqkv, ab, conv_w, a_log, dt_bias, ones_bd, n_lat):
    b, tt, w = gqkv.shape
    nt = tt // TILE
    sub = 8
    per = TILE // sub
    last = tt // sub - 1
    pad = LANES - 2 * GDN_HEADS
    alog_pad = jnp.pad(a_log.reshape(1, 2 * GDN_HEADS), ((0, 0), (0, pad)))
    dtb_pad = jnp.pad(dt_bias.reshape(1, 2 * GDN_HEADS), ((0, 0), (0, pad)))
    return pl.pallas_call(
        functools.partial(_gdn_prep_kernel, n_lat),
        grid=(b, nt),
        in_specs=[pl.BlockSpec((1, TILE, w), lambda bi, t: (bi, t, 0)),
                  pl.BlockSpec((1, sub, w), lambda bi, t: (bi, jnp.maximum(t * per - 1, 0), 0)),
                  pl.BlockSpec((1, sub, w), lambda bi, t: (bi, jnp.minimum((t + 1) * per, last), 0)),
                  pl.BlockSpec((1, TILE, LANES), lambda bi, t: (bi, t, 0)),
                  pl.BlockSpec((CONV_W, w), lambda bi, t: (0, 0)),
                  pl.BlockSpec((1, LANES), lambda bi, t: (0, 0)),
                  pl.BlockSpec((1, LANES), lambda bi, t: (0, 0)),
                  pl.BlockSpec((GDN_W, GDN_W), lambda bi, t: (0, 0))],
        out_specs=[pl.BlockSpec((1, TILE, w), lambda bi, t: (bi, t, 0)),
                   pl.BlockSpec((1, TILE, LANES), lambda bi, t: (bi, t, 0))],
        out_shape=[jax.ShapeDtypeStruct((b, tt, w), F32), jax.ShapeDtypeStruct((b, tt, LANES), F32)],
        compiler_params=_cparams(2),
        name="gdn_prep",
    )(gqkv, gqkv, gqkv, ab, conv_w, alog_pad, dtb_pad, ones_bd)


N_SUB = TILE // GDN_CHUNK
N_DOUBLINGS = 5


def _gdn_kernel(n_lat, nw_ref, ones_ref, tri_ref, eg_ref, eb_ref, x_ref, gb_ref, z_ref,
                y_ref, s_ref, o_ref):
    p = pl.program_id(1)
    c = pl.program_id(2)
    chunk = _chunk_of(p, c, n_lat)
    rows = pl.ds(pl.multiple_of(chunk * TILE, TILE), TILE)

    @pl.when(c == 0)
    def _():
        s_ref[...] = jnp.zeros_like(s_ref)

    blk = x_ref[0]
    qn = blk[:, 0:GDN_W]
    kn = blk[:, GDN_W:2 * GDN_W]
    v = blk[:, 2 * GDN_W:3 * GDN_W]
    gbv = gb_ref[0]
    ones_bd = ones_ref[...]
    masks = _head_masks(TILE, GDN_W)
    sub_masks = _head_masks(GDN_CHUNK, GDN_W)
    ri = lax.broadcasted_iota(jnp.int32, (TILE, TILE), 0)
    ci = lax.broadcasted_iota(jnp.int32, (TILE, TILE), 1)
    bd = (ri // GDN_CHUNK) == (ci // GDN_CHUNK)
    eye = (ri == ci).astype(F32)
    kn_b = kn.astype(BF16)

    def sweep(direction):
        gc = _dot_sel_x(tri_ref[direction], gbv)
        gtot = _dot_sel_x(ones_bd, gbv)
        eg = eg_ref[direction]
        gc_e = _dot_x_sel(gc, eg)
        gtot_e = _dot_x_sel(gtot, eg)
        beta_e = _dot_x_sel(gbv, eb_ref[direction])
        egc = jnp.exp(gc_e)
        vb = v * beta_e
        kb = kn * beta_e * egc
        qd = qn * egc
        kt = kn * jnp.exp(gtot_e - gc_e)
        gtot_dec = jnp.exp(gtot_e)
        gc_t = gc.T
        causal = bd & ((ri >= ci) if direction == 0 else (ri <= ci))

        u = jnp.zeros((TILE, GDN_W), F32)
        w = jnp.zeros((TILE, GDN_W), F32)
        attn = []
        for h in range(GDN_HEADS):
            r = direction * GDN_HEADS + h
            kk = _dot_nt(jnp.where(masks[h], kn, 0.0).astype(BF16), kn_b)
            qk = _dot_nt(jnp.where(masks[h], qn, 0.0).astype(BF16), kn_b)
            decay = jnp.exp(jnp.where(causal, gc[:, r:r + 1] - gc_t[r:r + 1, :], NEG_INF))
            bcol = gbv[:, 2 * GDN_HEADS + r:2 * GDN_HEADS + r + 1]
            m = jnp.where(ri == ci, 0.0, kk * bcol * decay)
            attn.append(qk * decay)
            pw = m
            inv = eye - m
            for _ in range(N_DOUBLINGS):
                pb = pw.astype(BF16)
                pw = _dot(pb, pb)
                inv = inv + _dot(inv.astype(BF16), pw.astype(BF16))
            inv_b = inv.astype(BF16)
            u = u + _dot(inv_b, jnp.where(masks[h], vb, 0.0).astype(BF16))
            w = w + _dot(inv_b, jnp.where(masks[h], kb, 0.0).astype(BF16))

        kt_t = kt.T
        outs = [None] * N_SUB
        order = range(N_SUB) if direction == 0 else range(N_SUB - 1, -1, -1)
        s = s_ref[...]
        for sub in order:
            r0 = sub * GDN_CHUNK
            sb = s.astype(BF16)
            wq = jnp.concatenate([w[r0:r0 + GDN_CHUNK], qd[r0:r0 + GDN_CHUNK]], axis=0).astype(BF16)
            ws = _dot(wq, sb)
            v_new = u[r0:r0 + GDN_CHUNK] - ws[0:GDN_CHUNK]
            o_sub = ws[GDN_CHUNK:2 * GDN_CHUNK]
            for h in range(GDN_HEADS):
                a_blk = attn[h][r0:r0 + GDN_CHUNK, r0:r0 + GDN_CHUNK].astype(BF16)
                o_sub = o_sub + _dot(a_blk, jnp.where(sub_masks[h], v_new, 0.0).astype(BF16))
            outs[sub] = o_sub
            upd = _dot(kt_t[:, r0:r0 + GDN_CHUNK].astype(BF16), v_new.astype(BF16))
            s = s * gtot_dec[r0:r0 + 1, :] + jnp.where(
                (ri // HEAD_DIM) == (ci // HEAD_DIM), upd, 0.0)
        s_ref[...] = s
        return jnp.concatenate(outs, axis=0)

    @pl.when(p == 0)
    def _():
        o_ref[rows, :] = sweep(0)

    @pl.when(p == 1)
    def _():
        o = o_ref[rows, :] + sweep(1)
        ms = _head_sum(o * o, ones_bd) * (1.0 / HEAD_DIM)
        y_ref[0] = o * lax.rsqrt(ms + EPS) * nw_ref[...] * _silu(z_ref[0])


def _gdn_consts():
    i = np.arange(TILE)
    same = (i[:, None] // GDN_CHUNK) == (i[None, :] // GDN_CHUNK)
    tri = np.stack([same & (i[None, :] <= i[:, None]), same & (i[None, :] >= i[:, None])])
    eg = np.zeros((2, LANES, GDN_W), np.float32)
    eb = np.zeros((2, LANES, GDN_W), np.float32)
    for d in range(2):
        for h in range(GDN_HEADS):
            eg[d, d * GDN_HEADS + h, h * HEAD_DIM:(h + 1) * HEAD_DIM] = 1.0
            eb[d, 2 * GDN_HEADS + d * GDN_HEADS + h, h * HEAD_DIM:(h + 1) * HEAD_DIM] = 1.0
    return (jnp.asarray(same, BF16), jnp.asarray(tri, BF16), jnp.asarray(eg, BF16), jnp.asarray(eb, BF16))


def _gdn_call(qkvn, gb, gz, gdn_norm_w, consts, n_lat):
    b, tt, w = qkvn.shape
    nt = tt // TILE
    ones_chunk, tri, eg, eb = consts

    def in_idx(bi, p, c):
        return (bi, _chunk_of(p, c, n_lat), 0)

    def out_idx(bi, p, c):
        return (bi, jnp.where(p == 0, n_lat, _chunk_of(p, c, n_lat)), 0)

    const2 = lambda bi, p, c: (0, 0)
    const3 = lambda bi, p, c: (0, 0, 0)
    return pl.pallas_call(
        functools.partial(_gdn_kernel, n_lat),
        grid=(b, 2, nt),
        in_specs=[pl.BlockSpec((1, GDN_W), const2),
                  pl.BlockSpec((TILE, TILE), const2),
                  pl.BlockSpec((2, TILE, TILE), const3),
                  pl.BlockSpec((2, LANES, GDN_W), const3),
                  pl.BlockSpec((2, LANES, GDN_W), const3),
                  pl.BlockSpec((1, TILE, w), in_idx),
                  pl.BlockSpec((1, TILE, LANES), in_idx),
                  pl.BlockSpec((1, TILE, GDN_W), in_idx)],
        out_specs=pl.BlockSpec((1, TILE, GDN_W), out_idx),
        out_shape=jax.ShapeDtypeStruct((b, tt, GDN_W), F32),
        scratch_shapes=[pltpu.VMEM((GDN_W, GDN_W), F32), pltpu.VMEM((tt, GDN_W), F32)],
        compiler_params=_cparams(3),
        name="gdn",
    )(jnp.tile(gdn_norm_w, GDN_HEADS).reshape(1, GDN_W), ones_chunk, tri, eg, eb, qkvn, gb, gz)


FF_CHUNK = 512


def _post_kernel(d_ff, final, x_ref, yr_ref, ys_ref, yg_ref, g1_ref, sh_ref, sc_ref, g2_ref, nw_ref,
                 fw_ref, wo_ref, w1_ref, w2_ref, o_ref):
    mix = (_dot(yr_ref[0].astype(BF16), wo_ref[0:RET_W, :])
           + _dot(ys_ref[0].astype(BF16), wo_ref[RET_W:RET_W + SWA_W, :])
           + _dot(yg_ref[0].astype(BF16), wo_ref[RET_W + SWA_W:, :]))
    x1 = x_ref[0] + g1_ref[0] * mix
    y = x1 * lax.rsqrt(jnp.mean(x1 * x1, axis=-1, keepdims=True) + EPS) * nw_ref[...]
    h = (y * (1.0 + sc_ref[0]) + sh_ref[0]).astype(BF16)
    acc = jnp.zeros_like(x1)
    for c0 in range(0, d_ff, FF_CHUNK):
        cw = min(FF_CHUNK, d_ff - c0)
        gate = _dot(h, w1_ref[:, c0:c0 + cw])
        up = _dot(h, w1_ref[:, d_ff + c0:d_ff + c0 + cw])
        acc = acc + _dot((_silu(gate) * up).astype(BF16), w2_ref[c0:c0 + cw, :])
    x2 = x1 + g2_ref[0] * acc
    if final:
        x2 = x2 * lax.rsqrt(jnp.mean(x2 * x2, axis=-1, keepdims=True) + EPS) * fw_ref[...]
    o_ref[0] = x2


def _post_call(xcat, y_r, y_s, y_g, mod, norm_w, final_w, wo_bf, w1_bf, w2_bf, n_lat_tiles, ctx_row,
               n_tiles, final):
    b, tt, d = xcat.shape
    d_ff = w2_bf.shape[0]

    def mod_row(bi, t):
        return jnp.where(t >= n_lat_tiles, ctx_row, bi)

    def mod_spec(k):
        return pl.BlockSpec((1, 1, d), lambda bi, t: (mod_row(bi, t), 0, k))

    def tile_spec(w):
        return pl.BlockSpec((1, TILE, w), lambda bi, t: (bi, t, 0))

    full = lambda shape: pl.BlockSpec(shape, lambda bi, t: (0,) * len(shape))
    return pl.pallas_call(
        functools.partial(_post_kernel, d_ff, final),
        grid=(b, n_tiles),
        in_specs=[tile_spec(d), tile_spec(RET_W), tile_spec(SWA_W), tile_spec(GDN_W),
                  mod_spec(2), mod_spec(3), mod_spec(4), mod_spec(5),
                  full((1, d)), full((1, d)),
                  full(wo_bf.shape), full(w1_bf.shape), full(w2_bf.shape)],
        out_specs=tile_spec(d),
        out_shape=jax.ShapeDtypeStruct((b, n_tiles * TILE, d), F32),
        compiler_params=_cparams(2),
        name="out_proj_ffn",
    )(xcat, y_r, y_s, y_g, mod, mod, mod, mod, norm_w.reshape(1, d), final_w.reshape(1, d),
      wo_bf, w1_bf, w2_bf)


def _rope_tables(n_lat_rows, n_ctx_rows):
    pos = np.arange(n_lat_rows)
    lane = np.arange(LANES) % HEAD_DIM
    axis = lane // (2 * N_FREQ)
    second_half = (lane % (2 * N_FREQ)) >= N_FREQ
    freqs = jnp.asarray(ROPE_BASE, F32) ** (-jnp.arange(N_FREQ, dtype=F32) / N_FREQ)
    coord = jnp.where(jnp.asarray(axis == 0)[None, :],
                      jnp.asarray(pos // GRID_W, F32)[:, None],
                      jnp.asarray(pos % GRID_W, F32)[:, None])
    ang = coord * freqs[jnp.asarray(lane % N_FREQ)][None, :]
    cos = jnp.cos(ang)
    sin = jnp.where(jnp.asarray(second_half)[None, :], jnp.sin(ang), -jnp.sin(ang))
    cos = jnp.concatenate([cos, jnp.ones((n_ctx_rows, LANES), F32)], axis=0)
    sin = jnp.concatenate([sin, jnp.zeros((n_ctx_rows, LANES), F32)], axis=0)
    return cos, sin


def kernel(x, c, ctx, c_ctx, ada_w, ada_b, norm_mix_w, norm_ffn_w, w_in, ret_rate, ret_norm_w,
           swa_sinks, gdn_conv_w, gdn_a_log, gdn_dt_bias, gdn_norm_w, w_out, w_ffn_in, w_ffn_out,
           final_norm_w):
    b, l, d = x.shape
    lc = ctx.shape[1]
    depth = ada_w.shape[0]
    assert l % TILE == 0 and lc == TILE and b < MOD_ROWS and d % LANES == 0
    n_lat = l // TILE
    n_tiles = n_lat + lc // TILE

    xcat = jnp.concatenate([x, ctx], axis=1)
    c_pad = jnp.zeros((MOD_ROWS, d), F32).at[:b].set(c).at[b].set(c_ctx)
    mod = _ada_call(c_pad, ada_w, ada_b)
    cos_t, sin_t = _rope_tables(l, lc)
    consts = _gdn_consts()
    i = np.arange(RET_W)
    ones_head = jnp.asarray((i[:, None] // HEAD_DIM) == (i[None, :] // HEAD_DIM), BF16)
    log_gamma = jnp.log1p(-jnp.exp2(-ret_rate.astype(F32)))

    for layer in range(depth):
        last = layer == depth - 1
        mod_l = mod[layer].reshape(MOD_ROWS, 1, N_MOD * d)
        w_in_bf = jnp.pad(w_in[layer], ((0, 0), (0, IN_W_PAD - IN_W))).astype(BF16)
        ret, swq, swkv, gqkv, gz, ab = _inproj_call(xcat, mod_l, norm_mix_w[layer], w_in_bf,
                                                    cos_t, sin_t, n_lat, b)
        y_r = _ret_call(ret, log_gamma[layer], ret_norm_w[layer], ones_head, n_lat)
        n_lat_blk = l // Q_BLOCK
        n_q_blk = n_lat_blk if last else n_lat_blk + lc // Q_BLOCK
        y_s = _swa_call(swq, swkv, swa_sinks[layer], n_lat_blk, n_q_blk)
        qkvn, gb = _gdn_prep_call(gqkv, ab, gdn_conv_w[layer], gdn_a_log[layer], gdn_dt_bias[layer],
                                  ones_head, n_lat)
        y_g = _gdn_call(qkvn, gb, gz, gdn_norm_w[layer], consts, n_lat)
        xcat = _post_call(xcat, y_r, y_s, y_g, mod_l, norm_ffn_w[layer], final_norm_w,
                          w_out[layer].astype(BF16), w_ffn_in[layer].astype(BF16),
                          w_ffn_out[layer].astype(BF16), n_lat, b,
                          n_lat if last else n_tiles, last)
    return xcat
```

```python
import functools

import numpy as np
import jax
import jax.numpy as jnp
from jax import lax
from jax.experimental import pallas as pl
from jax.experimental.pallas import tpu as pltpu

F32 = jnp.float32
BF16 = jnp.bfloat16

HEAD_DIM = 64
N_FREQ = HEAD_DIM // 4
GRID_W = 64
RET_HEADS = 4
SWA_Q_HEADS = 8
SWA_KV_HEADS = 2
GDN_HEADS = 4
RET_W = RET_HEADS * HEAD_DIM
SWA_W = SWA_Q_HEADS * HEAD_DIM
SWA_KV_W = SWA_KV_HEADS * HEAD_DIM
GDN_W = GDN_HEADS * HEAD_DIM
WINDOW = 128
Q_BLOCK = 128
GDN_CHUNK = 64
CONV_W = 3
ROPE_BASE = 10000.0
EPS = 1e-6
NEG_INF = -1e30
N_MOD = 6

LANES = 128
TILE = 256
MOD_ROWS = 16
VMEM_LIMIT = 56 * 1024 * 1024

COL_RET = 0
COL_SWA_Q = 4 * RET_W
COL_SWA_KV = COL_SWA_Q + SWA_W
COL_GDN_QKV = COL_SWA_KV + 2 * SWA_KV_W
COL_GDN_Z = COL_GDN_QKV + 3 * GDN_W
COL_AB = COL_GDN_Z + GDN_W
IN_W = COL_AB + 4 * GDN_HEADS
IN_W_PAD = COL_AB + LANES


def _dot(a, b):
    return jnp.dot(a, b, preferred_element_type=F32)


def _dot_nt(a, b):
    return lax.dot_general(a, b, (((1,), (1,)), ((), ())), preferred_element_type=F32)


def _split3(x):
    hi = x.astype(BF16)
    r = x - hi.astype(F32)
    mid = r.astype(BF16)
    lo = (r - mid.astype(F32)).astype(BF16)
    return hi, mid, lo


def _dot_x_sel(x, sel):
    hi, mid, lo = _split3(x)
    return _dot(hi, sel) + _dot(mid, sel) + _dot(lo, sel)


def _dot_sel_x(sel, x):
    hi, mid, lo = _split3(x)
    return _dot(sel, hi) + _dot(sel, mid) + _dot(sel, lo)


def _silu(x):
    return x * jax.nn.sigmoid(x)


def _head_masks(rows, width):
    lane = lax.broadcasted_iota(jnp.int32, (rows, width), 1)
    return [(lane // HEAD_DIM) == h for h in range(width // HEAD_DIM)]


def _head_sum(x, ones_bd):
    return _dot_x_sel(x, ones_bd)


def _cparams(n_axes):
    return pltpu.CompilerParams(dimension_semantics=("arbitrary",) * n_axes,
                                vmem_limit_bytes=VMEM_LIMIT)


def _ada_kernel(c_ref, w_ref, b_ref, o_ref):
    cs = _silu(c_ref[...])
    o_ref[0] = _dot(cs.astype(BF16), w_ref[0].astype(BF16)) + b_ref[0]


def _ada_call(c_pad, ada_w, ada_b):
    depth, d, n = ada_w.shape
    tn = 1536
    return pl.pallas_call(
        _ada_kernel,
        grid=(depth, n // tn),
        in_specs=[pl.BlockSpec((MOD_ROWS, d), lambda l, j: (0, 0)),
                  pl.BlockSpec((1, d, tn), lambda l, j: (l, 0, j)),
                  pl.BlockSpec((1, 1, tn), lambda l, j: (l, 0, j))],
        out_specs=pl.BlockSpec((1, MOD_ROWS, tn), lambda l, j: (l, 0, j)),
        out_shape=jax.ShapeDtypeStruct((depth, MOD_ROWS, n), F32),
        compiler_params=_cparams(2),
        name="ada_mod",
    )(c_pad, ada_w, ada_b.reshape(depth, 1, n))


def _rope(t, cos, sin_signed):
    lane = lax.broadcasted_iota(jnp.int32, t.shape, 1)
    first_half = (lane % (2 * N_FREQ)) < N_FREQ
    partner = jnp.where(first_half, pltpu.roll(t, LANES - N_FREQ, 1), pltpu.roll(t, N_FREQ, 1))
    return t * cos + partner * sin_signed


def _inproj_kernel(x_ref, sh_ref, sc_ref, nw_ref, w_ref, cos_ref, sin_ref,
                   ret_ref, swq_ref, swkv_ref, gqkv_ref, gz_ref, ab_ref):
    x = x_ref[0]
    y = x * lax.rsqrt(jnp.mean(x * x, axis=-1, keepdims=True) + EPS) * nw_ref[...]
    h = (y * (1.0 + sc_ref[0]) + sh_ref[0]).astype(BF16)
    cos = cos_ref[...]
    sin = sin_ref[...]

    def proj(col, width, out_ref, out_col, roped_lanes):
        r = _dot(h, w_ref[:, col:col + width])
        for g in range(width // LANES):
            rg = r[:, g * LANES:(g + 1) * LANES]
            if g * LANES < roped_lanes:
                rg = _rope(rg, cos, sin)
            out_ref[0, :, out_col + g * LANES:out_col + (g + 1) * LANES] = rg

    for g in range(4):
        proj(COL_RET + g * RET_W, RET_W, ret_ref, g * RET_W, RET_W if g < 2 else 0)
    for g in range(SWA_W // TILE):
        proj(COL_SWA_Q + g * TILE, TILE, swq_ref, g * TILE, TILE)
    proj(COL_SWA_KV, 2 * SWA_KV_W, swkv_ref, 0, SWA_KV_W)
    for g in range(3):
        proj(COL_GDN_QKV + g * GDN_W, GDN_W, gqkv_ref, g * GDN_W, 0)
    proj(COL_GDN_Z, GDN_W, gz_ref, 0, 0)
    proj(COL_AB, LANES, ab_ref, 0, 0)


def _inproj_call(xcat, mod, norm_w, w_in_bf, cos_t, sin_t, n_lat_tiles, ctx_row):
    b, tt, d = xcat.shape
    nt = tt // TILE

    def mod_row(bi, t):
        return jnp.where(t >= n_lat_tiles, ctx_row, bi)

    widths = (4 * RET_W, SWA_W, 2 * SWA_KV_W, 3 * GDN_W, GDN_W, LANES)
    return pl.pallas_call(
        _inproj_kernel,
        grid=(b, nt),
        in_specs=[pl.BlockSpec((1, TILE, d), lambda bi, t: (bi, t, 0)),
                  pl.BlockSpec((1, 1, d), lambda bi, t: (mod_row(bi, t), 0, 0)),
                  pl.BlockSpec((1, 1, d), lambda bi, t: (mod_row(bi, t), 0, 1)),
                  pl.BlockSpec((1, d), lambda bi, t: (0, 0)),
                  pl.BlockSpec((d, IN_W_PAD), lambda bi, t: (0, 0)),
                  pl.BlockSpec((TILE, LANES), lambda bi, t: (t, 0)),
                  pl.BlockSpec((TILE, LANES), lambda bi, t: (t, 0))],
        out_specs=[pl.BlockSpec((1, TILE, w), lambda bi, t: (bi, t, 0)) for w in widths],
        out_shape=[jax.ShapeDtypeStruct((b, tt, w), F32) for w in widths],
        compiler_params=_cparams(2),
        name="in_proj",
    )(xcat, mod, mod, norm_w.reshape(1, d), w_in_bf, cos_t, sin_t)


def _ret_kernel(lg_ref, lge_ref, rf_ref, rb_ref, of_ref, ob_ref, sf_ref, sb_ref):
    c = pl.program_id(1)

    @pl.when(c == 0)
    def _():
        sf_ref[...] = jnp.zeros_like(sf_ref)
        sb_ref[...] = jnp.zeros_like(sb_ref)

    masks = _head_masks(TILE, RET_W)
    ri = lax.broadcasted_iota(jnp.int32, (TILE, TILE), 0)
    ci = lax.broadcasted_iota(jnp.int32, (TILE, TILE), 1)
    bd = (ri // HEAD_DIM) == (ci // HEAD_DIM)
    idx = ri.astype(F32)

    def sweep(direction, r_ref, s_ref):
        blk = r_ref[0]
        q = blk[:, 0:RET_W] * (HEAD_DIM ** -0.5)
        k = blk[:, RET_W:2 * RET_W]
        v = blk[:, 2 * RET_W:3 * RET_W]
        lge = lge_ref[direction:direction + 1, :]
        if direction == 0:
            q_dec = jnp.exp(lge * (idx + 1.0))
            k_dec = jnp.exp(lge * (TILE - 1.0 - idx))
        else:
            q_dec = jnp.exp(lge * (TILE - idx))
            k_dec = jnp.exp(lge * idx)
        c_dec = jnp.exp(lge * float(TILE))
        s = s_ref[...]
        cross = _dot((q * q_dec).astype(BF16), s.astype(BF16))
        kt = (k * k_dec).T
        upd = _dot(kt.astype(BF16), v.astype(BF16))
        s_ref[...] = s * c_dec + jnp.where(bd, upd, 0.0)
        return cross, q, k, v

    o, q, k, v = sweep(0, rf_ref, sf_ref)
    diff = (ri - ci).astype(F32)
    kb = k.astype(BF16)
    for h in range(RET_HEADS):
        lgf = lg_ref[0, h]
        lgb = lg_ref[1, h]
        dm = (jnp.where(diff >= 0, jnp.exp(lgf * jnp.maximum(diff, 0.0)), 0.0)
              + jnp.where(diff <= 0, jnp.exp(lgb * jnp.maximum(-diff, 0.0)), 0.0))
        sc = _dot_nt(jnp.where(masks[h], q, 0.0).astype(BF16), kb)
        o = o + _dot((sc * dm).astype(BF16), jnp.where(masks[h], v, 0.0).astype(BF16))
    of_ref[0] = o
    ob_ref[0] = sweep(1, rb_ref, sb_ref)[0]


def _ret_call(ret, log_gamma, n_lat):
    b, tt, _ = ret.shape
    nt = tt // TILE
    lge = jnp.repeat(log_gamma, HEAD_DIM, axis=-1)

    def f_idx(bi, c):
        return (bi, jnp.where(c == 0, n_lat, c - 1), 0)

    def b_idx(bi, c):
        return (bi, jnp.where(c == 0, n_lat, n_lat - c), 0)

    return pl.pallas_call(
        _ret_kernel,
        grid=(b, nt),
        in_specs=[pl.BlockSpec(memory_space=pltpu.SMEM),
                  pl.BlockSpec((2, RET_W), lambda bi, c: (0, 0)),
                  pl.BlockSpec((1, TILE, 3 * RET_W), f_idx),
                  pl.BlockSpec((1, TILE, 3 * RET_W), b_idx)],
        out_specs=[pl.BlockSpec((1, TILE, RET_W), f_idx), pl.BlockSpec((1, TILE, RET_W), b_idx)],
        out_shape=[jax.ShapeDtypeStruct((b, tt, RET_W), F32)] * 2,
        scratch_shapes=[pltpu.VMEM((RET_W, RET_W), F32), pltpu.VMEM((RET_W, RET_W), F32)],
        compiler_params=_cparams(2),
        name="retention",
    )(log_gamma, lge, ret, ret)


def _swa_kernel(n_lat_blk, sink_ref, q_ref, kv_ref, y_ref):
    j = pl.program_id(1)
    is_lat = j < n_lat_blk
    jc = jnp.minimum(j, n_lat_blk - 1)
    jp = jnp.maximum(jc - 1, 0)
    jn = jnp.minimum(jc + 1, n_lat_blk - 1)
    lat_rows = n_lat_blk * Q_BLOCK

    def piece(blk):
        return kv_ref[0, pl.ds(pl.multiple_of(blk * Q_BLOCK, Q_BLOCK), Q_BLOCK), :]

    kv_all = jnp.concatenate([piece(jp), piece(jc), piece(jn), kv_ref[0, lat_rows:, :]], axis=0)
    k_all = kv_all[:, 0:SWA_KV_W].astype(BF16)
    v_all = kv_all[:, SWA_KV_W:2 * SWA_KV_W].astype(BF16)
    n_piece = k_all.shape[0] // Q_BLOCK

    group = SWA_Q_HEADS // SWA_KV_HEADS
    rows = group * Q_BLOCK
    ql = lax.broadcasted_iota(jnp.int32, (rows, Q_BLOCK), 0) % Q_BLOCK
    kl = lax.broadcasted_iota(jnp.int32, (rows, Q_BLOCK), 1)
    ok_prev = (kl >= ql) & is_lat & (j >= 1)
    ok_next = (kl <= ql) & is_lat & (j <= n_lat_blk - 2)
    lane = lax.broadcasted_iota(jnp.int32, (Q_BLOCK, LANES), 1)
    low = lane < HEAD_DIM
    out_tiles = [[None, None] for _ in range(SWA_W // LANES)]

    for g in range(SWA_KV_HEADS):
        q_rows, sink_rows = [], []
        for hh in range(group):
            head = g * group + hh
            t, e = head // 2, head % 2
            qt = q_ref[0, :, t * LANES:(t + 1) * LANES] * (HEAD_DIM ** -0.5)
            qm = jnp.where(low if e == 0 else ~low, qt, 0.0)
            if e != g:
                qm = pltpu.roll(qm, HEAD_DIM, 1)
            q_rows.append(qm)
            sink_rows.append(jnp.full((Q_BLOCK, 1), sink_ref[0, head], F32))
        qs = jnp.concatenate(q_rows, axis=0).astype(BF16)
        sink = jnp.concatenate(sink_rows, axis=0)
        s = _dot_nt(qs, k_all)

        def masked(i):
            s_i = s[:, i * Q_BLOCK:(i + 1) * Q_BLOCK]
            if i == 0:
                return jnp.where(ok_prev, s_i, NEG_INF)
            if i == 1:
                return jnp.where(is_lat, s_i, NEG_INF)
            if i == 2:
                return jnp.where(ok_next, s_i, NEG_INF)
            return s_i

        mx = masked(0)
        for i in range(1, n_piece):
            mx = jnp.maximum(mx, masked(i))
        m = jnp.maximum(jnp.max(mx, axis=-1, keepdims=True), sink)
        pr = [jnp.exp(masked(i) - m) for i in range(n_piece)]
        acc = pr[0]
        for p_i in pr[1:]:
            acc = acc + p_i
        den = jnp.sum(acc, axis=-1, keepdims=True) + jnp.exp(sink - m)
        pb = jnp.concatenate([p_i.astype(BF16) for p_i in pr], axis=1)
        o = _dot(pb, v_all) / den
        for hh in range(group):
            head = g * group + hh
            t, e = head // 2, head % 2
            oh = o[hh * Q_BLOCK:(hh + 1) * Q_BLOCK]
            if e != g:
                oh = pltpu.roll(oh, HEAD_DIM, 1)
            out_tiles[t][e] = oh
    for t in range(SWA_W // LANES):
        y_ref[0, :, t * LANES:(t + 1) * LANES] = jnp.where(low, out_tiles[t][0], out_tiles[t][1])


def _swa_call(swq, swkv, sinks, n_lat_blk, n_q_blk):
    b, tt, _ = swq.shape
    return pl.pallas_call(
        functools.partial(_swa_kernel, n_lat_blk),
        grid=(b, n_q_blk),
        in_specs=[pl.BlockSpec(memory_space=pltpu.SMEM),
                  pl.BlockSpec((1, Q_BLOCK, SWA_W), lambda bi, j: (bi, j, 0)),
                  pl.BlockSpec((1, tt, 2 * SWA_KV_W), lambda bi, j: (bi, 0, 0))],
        out_specs=pl.BlockSpec((1, Q_BLOCK, SWA_W), lambda bi, j: (bi, j, 0)),
        out_shape=jax.ShapeDtypeStruct((b, tt, SWA_W), F32),
        compiler_params=_cparams(2),
        name="swa",
    )(sinks.reshape(1, SWA_Q_HEADS), swq, swkv)


def _gdn_prep_kernel(n_lat, x_ref, hp_ref, hn_ref, ab_ref, cw_ref, alog_ref, dtb_ref, ones_ref,
                     qkv_ref, gb_ref):
    t = pl.program_id(1)
    x = x_ref[0]
    has_prev = ((t > 0) & (t < n_lat)).astype(F32)
    has_next = (t < n_lat - 1).astype(F32)
    row = lax.broadcasted_iota(jnp.int32, x.shape, 0)
    prev_row = hp_ref[0, 7:8, :] * has_prev
    next_row = hn_ref[0, 0:1, :] * has_next
    xm1 = jnp.where(row == 0, prev_row, pltpu.roll(x, 1, 0))
    xp1 = jnp.where(row == TILE - 1, next_row, pltpu.roll(x, TILE - 1, 0))
    conv = xm1 * cw_ref[0:1, :] + x * cw_ref[1:2, :] + xp1 * cw_ref[2:3, :]
    s = _silu(conv)
    ones_bd = ones_ref[...]

    def l2n(u):
        return u * lax.rsqrt(_head_sum(u * u, ones_bd) + EPS)

    qkv_ref[0, :, 0:GDN_W] = l2n(s[:, 0:GDN_W]) * (HEAD_DIM ** -0.5)
    qkv_ref[0, :, GDN_W:2 * GDN_W] = l2n(s[:, GDN_W:2 * GDN_W])
    qkv_ref[0, :, 2 * GDN_W:3 * GDN_W] = s[:, 2 * GDN_W:3 * GDN_W]

    ab = ab_ref[0]
    lane = lax.broadcasted_iota(jnp.int32, ab.shape, 1)
    g = -jnp.exp(alog_ref[...]) * jax.nn.softplus(ab + dtb_ref[...])
    beta = jax.nn.sigmoid(ab)
    gb_ref[0] = jnp.where(lane < 2 * GDN_HEADS, g, jnp.where(lane < 4 * GDN_HEADS, beta, 0.0))


def _gdn_prep_call(gqkv, ab, conv_w, a_log, dt_bias, ones_bd, n_lat):
    b, tt, w = gqkv.shape
    nt = tt // TILE
    sub = 8
    per = TILE // sub
    last = tt // sub - 1
    pad = LANES - 2 * GDN_HEADS
    alog_pad = jnp.pad(a_log.reshape(1, 2 * GDN_HEADS), ((0, 0), (0, pad)))
    dtb_pad = jnp.pad(dt_bias.reshape(1, 2 * GDN_HEADS), ((0, 0), (0, pad)))
    return pl.pallas_call(
        functools.partial(_gdn_prep_kernel, n_lat),
        grid=(b, nt),
        in_specs=[pl.BlockSpec((1, TILE, w), lambda bi, t: (bi, t, 0)),
                  pl.BlockSpec((1, sub, w), lambda bi, t: (bi, jnp.maximum(t * per - 1, 0), 0)),
                  pl.BlockSpec((1, sub, w), lambda bi, t: (bi, jnp.minimum((t + 1) * per, last), 0)),
                  pl.BlockSpec((1, TILE, LANES), lambda bi, t: (bi, t, 0)),
                  pl.BlockSpec((CONV_W, w), lambda bi, t: (0, 0)),
                  pl.BlockSpec((1, LANES), lambda bi, t: (0, 0)),
                  pl.BlockSpec((1, LANES), lambda bi, t: (0, 0)),
                  pl.BlockSpec((GDN_W, GDN_W), lambda bi, t: (0, 0))],
        out_specs=[pl.BlockSpec((1, TILE, w), lambda bi, t: (bi, t, 0)),
                   pl.BlockSpec((1, TILE, LANES), lambda bi, t: (bi, t, 0))],
        out_shape=[jax.ShapeDtypeStruct((b, tt, w), F32), jax.ShapeDtypeStruct((b, tt, LANES), F32)],
        compiler_params=_cparams(2),
        name="gdn_prep",
    )(gqkv, gqkv, gqkv, ab, conv_w, alog_pad, dtb_pad, ones_bd)


N_SUB = TILE // GDN_CHUNK


def _gdn_kernel(n_lat, ones_ref, tri_ref, xf_ref, gbf_ref, xb_ref, gbb_ref,
                of_ref, ob_ref, sf_ref, sb_ref):
    c = pl.program_id(1)

    @pl.when(c == 0)
    def _():
        sf_ref[...] = jnp.zeros_like(sf_ref)
        sb_ref[...] = jnp.zeros_like(sb_ref)

    ones_bd = ones_ref[...]
    masks = _head_masks(TILE, GDN_W)
    sub_masks = _head_masks(GDN_CHUNK, GDN_W)
    ri = lax.broadcasted_iota(jnp.int32, (TILE, TILE), 0)
    ci = lax.broadcasted_iota(jnp.int32, (TILE, TILE), 1)
    bd = (ri // GDN_CHUNK) == (ci // GDN_CHUNK)
    x_refs = (xf_ref, xb_ref)
    gb_refs = (gbf_ref, gbb_ref)
    s_refs = (sf_ref, sb_ref)
    o_refs = (of_ref, ob_ref)
    dirs = (0, 1)

    def expand(x, first_col):
        e = jnp.zeros((TILE, GDN_W), F32)
        for h in range(GDN_HEADS):
            e = jnp.where(masks[h], x[:, first_col + h:first_col + h + 1], e)
        return e

    pre = []
    for d in dirs:
        blk = x_refs[d][0]
        qn = blk[:, 0:GDN_W]
        kn = blk[:, GDN_W:2 * GDN_W]
        v = blk[:, 2 * GDN_W:3 * GDN_W]
        gbv = gb_refs[d][0]
        gc = _dot_sel_x(tri_ref[d], gbv)
        gtot = _dot_sel_x(ones_bd, gbv)
        gc_e = expand(gc, d * GDN_HEADS)
        gtot_e = expand(gtot, d * GDN_HEADS)
        beta_e = expand(gbv, 2 * GDN_HEADS + d * GDN_HEADS)
        egc = jnp.exp(gc_e)
        pre.append(dict(qn=qn, kn=kn, kn_b=kn.astype(BF16), gb_t=gbv.T, gc=gc, gc_t=gc.T,
                        vb_t=(v * beta_e).T, kb_t=(kn * beta_e * egc).T, qd=qn * egc,
                        kt=kn * jnp.exp(gtot_e - gc_e), gtot_dec=jnp.exp(gtot_e),
                        causal=bd & ((ri >= ci) if d == 0 else (ri <= ci)),
                        causal_t=bd & ((ri <= ci) if d == 0 else (ri >= ci))))

    chains = [(d, h) for h in range(GDN_HEADS) for d in dirs]
    m_t, z_acc, attn = {}, {}, {}
    for (d, h) in chains:
        p = pre[d]
        r = d * GDN_HEADS + h
        kk = _dot_nt(jnp.where(masks[h], p["kn"], 0.0).astype(BF16), p["kn_b"])
        qk = _dot_nt(jnp.where(masks[h], p["qn"], 0.0).astype(BF16), p["kn_b"])
        diff = p["gc"][:, r:r + 1] - p["gc_t"][r:r + 1, :]
        attn[(d, h)] = qk * jnp.exp(jnp.where(p["causal"], diff, NEG_INF))
        brow = p["gb_t"][2 * GDN_HEADS + r:2 * GDN_HEADS + r + 1, :]
        m_t[(d, h)] = jnp.where(ri == ci, 0.0,
                                kk * brow * jnp.exp(jnp.where(p["causal_t"], -diff, NEG_INF)))
        z_acc[(d, h)] = jnp.concatenate([p["vb_t"][h * HEAD_DIM:(h + 1) * HEAD_DIM],
                                         p["kb_t"][h * HEAD_DIM:(h + 1) * HEAD_DIM]], axis=0)

    def off_diag(s_blk):
        return ((ri // (2 * s_blk)) == (ci // (2 * s_blk))) & ((ri // s_blk) != (ci // s_blk))

    x_inv = {}
    for ch in chains:
        x_inv[ch] = (ri == ci).astype(F32) - jnp.where(off_diag(1), m_t[ch], 0.0)
    s_blk = 2
    while s_blk < GDN_CHUNK // 2:
        for ch in chains:
            xb = x_inv[ch].astype(BF16)
            pc = _dot(xb, jnp.where(off_diag(s_blk), m_t[ch], 0.0).astype(BF16))
            x_inv[ch] = x_inv[ch] - _dot(pc.astype(BF16), xb)
        s_blk *= 2
    for ch in chains:
        xb = x_inv[ch].astype(BF16)
        y = _dot(z_acc[ch].astype(BF16), xb)
        yc = _dot(y.astype(BF16), jnp.where(off_diag(s_blk), m_t[ch], 0.0).astype(BF16))
        z_acc[ch] = y - _dot(yc.astype(BF16), xb)

    u = [jnp.concatenate([z_acc[(d, h)][0:HEAD_DIM] for h in range(GDN_HEADS)], axis=0).T for d in dirs]
    w = [jnp.concatenate([z_acc[(d, h)][HEAD_DIM:2 * HEAD_DIM] for h in range(GDN_HEADS)], axis=0).T
         for d in dirs]

    state_bd = (ri // HEAD_DIM) == (ci // HEAD_DIM)
    kt_t = [pre[d]["kt"].T for d in dirs]
    s = [s_refs[d][...] for d in dirs]
    outs = [[None] * N_SUB for _ in dirs]
    for i in range(N_SUB):
        for d in dirs:
            sub = i if d == 0 else N_SUB - 1 - i
            r0 = sub * GDN_CHUNK
            wq = jnp.concatenate([w[d][r0:r0 + GDN_CHUNK], pre[d]["qd"][r0:r0 + GDN_CHUNK]], axis=0)
            ws = _dot(wq.astype(BF16), s[d].astype(BF16))
            v_new = u[d][r0:r0 + GDN_CHUNK] - ws[0:GDN_CHUNK]
            o_sub = ws[GDN_CHUNK:2 * GDN_CHUNK]
            for h in range(GDN_HEADS):
                a_blk = attn[(d, h)][r0:r0 + GDN_CHUNK, r0:r0 + GDN_CHUNK].astype(BF16)
                o_sub = o_sub + _dot(a_blk, jnp.where(sub_masks[h], v_new, 0.0).astype(BF16))
            outs[d][sub] = o_sub
            upd = _dot(kt_t[d][:, r0:r0 + GDN_CHUNK].astype(BF16), v_new.astype(BF16))
            s[d] = s[d] * pre[d]["gtot_dec"][r0:r0 + 1, :] + jnp.where(state_bd, upd, 0.0)
    for d in dirs:
        s_refs[d][...] = s[d]
        o_refs[d][0] = jnp.concatenate(outs[d], axis=0)


def _gdn_consts():
    i = np.arange(TILE)
    same = (i[:, None] // GDN_CHUNK) == (i[None, :] // GDN_CHUNK)
    tri = np.stack([same & (i[None, :] <= i[:, None]), same & (i[None, :] >= i[:, None])])
    return jnp.asarray(same, BF16), jnp.asarray(tri, BF16)


def _gdn_call(qkvn, gb, consts, n_lat):
    b, tt, w = qkvn.shape
    nt = tt // TILE
    ones_chunk, tri = consts

    def f_idx(bi, c):
        return (bi, jnp.where(c == 0, n_lat, c - 1), 0)

    def b_idx(bi, c):
        return (bi, jnp.where(c == 0, n_lat, n_lat - c), 0)

    return pl.pallas_call(
        functools.partial(_gdn_kernel, n_lat),
        grid=(b, nt),
        in_specs=[pl.BlockSpec((TILE, TILE), lambda bi, c: (0, 0)),
                  pl.BlockSpec((2, TILE, TILE), lambda bi, c: (0, 0, 0)),
                  pl.BlockSpec((1, TILE, w), f_idx),
                  pl.BlockSpec((1, TILE, LANES), f_idx),
                  pl.BlockSpec((1, TILE, w), b_idx),
                  pl.BlockSpec((1, TILE, LANES), b_idx)],
        out_specs=[pl.BlockSpec((1, TILE, GDN_W), f_idx), pl.BlockSpec((1, TILE, GDN_W), b_idx)],
        out_shape=[jax.ShapeDtypeStruct((b, tt, GDN_W), F32)] * 2,
        scratch_shapes=[pltpu.VMEM((GDN_W, GDN_W), F32), pltpu.VMEM((GDN_W, GDN_W), F32)],
        compiler_params=_cparams(2),
        name="gdn",
    )(ones_chunk, tri, qkvn, gb, qkvn, gb)


FF_CHUNK = 512


def _post_kernel(d_ff, final, x_ref, rf_ref, rb_ref, rg_ref, ys_ref, gf_ref, gb_ref, gz_ref,
                 g1_ref, sh_ref, sc_ref, g2_ref, nw_ref, fw_ref, rnw_ref, gnw_ref, ones_ref,
                 wo_ref, w1_ref, w2_ref, o_ref):
    ones_head = ones_ref[...]

    def head_out(f_ref, b_ref, gate_ref, w_ref):
        o = f_ref[0] + b_ref[0]
        ms = _head_sum(o * o, ones_head) * (1.0 / HEAD_DIM)
        return (o * lax.rsqrt(ms + EPS) * w_ref[...] * _silu(gate_ref[0])).astype(BF16)

    mix = (_dot(head_out(rf_ref, rb_ref, rg_ref, rnw_ref), wo_ref[0:RET_W, :])
           + _dot(ys_ref[0].astype(BF16), wo_ref[RET_W:RET_W + SWA_W, :])
           + _dot(head_out(gf_ref, gb_ref, gz_ref, gnw_ref), wo_ref[RET_W + SWA_W:, :]))
    x1 = x_ref[0] + g1_ref[0] * mix
    y = x1 * lax.rsqrt(jnp.mean(x1 * x1, axis=-1, keepdims=True) + EPS) * nw_ref[...]
    h = (y * (1.0 + sc_ref[0]) + sh_ref[0]).astype(BF16)
    acc = jnp.zeros_like(x1)
    for c0 in range(0, d_ff, FF_CHUNK):
        cw = min(FF_CHUNK, d_ff - c0)
        gate = _dot(h, w1_ref[:, c0:c0 + cw])
        up = _dot(h, w1_ref[:, d_ff + c0:d_ff + c0 + cw])
        acc = acc + _dot((_silu(gate) * up).astype(BF16), w2_ref[c0:c0 + cw, :])
    x2 = x1 + g2_ref[0] * acc
    if final:
        x2 = x2 * lax.rsqrt(jnp.mean(x2 * x2, axis=-1, keepdims=True) + EPS) * fw_ref[...]
    o_ref[0] = x2


def _post_call(xcat, o_rf, o_rb, ret, y_s, o_gf, o_gb, gz, mod, norm_w, final_w, ret_norm_w, gdn_norm_w,
               ones_head, wo_bf, w1_bf, w2_bf, n_lat_tiles, ctx_row, n_tiles, final):
    b, tt, d = xcat.shape
    d_ff = w2_bf.shape[0]

    def mod_row(bi, t):
        return jnp.where(t >= n_lat_tiles, ctx_row, bi)

    def mod_spec(k):
        return pl.BlockSpec((1, 1, d), lambda bi, t: (mod_row(bi, t), 0, k))

    def tile_spec(w):
        return pl.BlockSpec((1, TILE, w), lambda bi, t: (bi, t, 0))

    full = lambda shape: pl.BlockSpec(shape, lambda bi, t: (0,) * len(shape))
    return pl.pallas_call(
        functools.partial(_post_kernel, d_ff, final),
        grid=(b, n_tiles),
        in_specs=[tile_spec(d), tile_spec(RET_W), tile_spec(RET_W),
                  pl.BlockSpec((1, TILE, RET_W), lambda bi, t: (bi, t, 3)),
                  tile_spec(SWA_W), tile_spec(GDN_W), tile_spec(GDN_W), tile_spec(GDN_W),
                  mod_spec(2), mod_spec(3), mod_spec(4), mod_spec(5),
                  full((1, d)), full((1, d)), full((1, RET_W)), full((1, GDN_W)), full(ones_head.shape),
                  full(wo_bf.shape), full(w1_bf.shape), full(w2_bf.shape)],
        out_specs=tile_spec(d),
        out_shape=jax.ShapeDtypeStruct((b, n_tiles * TILE, d), F32),
        compiler_params=_cparams(2),
        name="out_proj_ffn",
    )(xcat, o_rf, o_rb, ret, y_s, o_gf, o_gb, gz, mod, mod, mod, mod, norm_w.reshape(1, d),
      final_w.reshape(1, d), ret_norm_w.reshape(1, RET_W),
      jnp.tile(gdn_norm_w, GDN_HEADS).reshape(1, GDN_W), ones_head, wo_bf, w1_bf, w2_bf)


def _rope_tables(n_lat_rows, n_ctx_rows):
    pos = np.arange(n_lat_rows)
    lane = np.arange(LANES) % HEAD_DIM
    axis = lane // (2 * N_FREQ)
    second_half = (lane % (2 * N_FREQ)) >= N_FREQ
    freqs = jnp.asarray(ROPE_BASE, F32) ** (-jnp.arange(N_FREQ, dtype=F32) / N_FREQ)
    coord = jnp.where(jnp.asarray(axis == 0)[None, :],
                      jnp.asarray(pos // GRID_W, F32)[:, None],
                      jnp.asarray(pos % GRID_W, F32)[:, None])
    ang = coord * freqs[jnp.asarray(lane % N_FREQ)][None, :]
    cos = jnp.cos(ang)
    sin = jnp.where(jnp.asarray(second_half)[None, :], jnp.sin(ang), -jnp.sin(ang))
    cos = jnp.concatenate([cos, jnp.ones((n_ctx_rows, LANES), F32)], axis=0)
    sin = jnp.concatenate([sin, jnp.zeros((n_ctx_rows, LANES), F32)], axis=0)
    return cos, sin


def kernel(x, c, ctx, c_ctx, ada_w, ada_b, norm_mix_w, norm_ffn_w, w_in, ret_rate, ret_norm_w,
           swa_sinks, gdn_conv_w, gdn_a_log, gdn_dt_bias, gdn_norm_w, w_out, w_ffn_in, w_ffn_out,
           final_norm_w):
    b, l, d = x.shape
    lc = ctx.shape[1]
    depth = ada_w.shape[0]
    assert l % TILE == 0 and lc == TILE and b < MOD_ROWS and d % LANES == 0
    n_lat = l // TILE
    n_tiles = n_lat + lc // TILE

    xcat = jnp.concatenate([x, ctx], axis=1)
    c_pad = jnp.zeros((MOD_ROWS, d), F32).at[:b].set(c).at[b].set(c_ctx)
    mod = _ada_call(c_pad, ada_w, ada_b)
    cos_t, sin_t = _rope_tables(l, lc)
    consts = _gdn_consts()
    i = np.arange(RET_W)
    ones_head = jnp.asarray((i[:, None] // HEAD_DIM) == (i[None, :] // HEAD_DIM), BF16)
    log_gamma = jnp.log1p(-jnp.exp2(-ret_rate.astype(F32)))

    for layer in range(depth):
        last = layer == depth - 1
        mod_l = mod[layer].reshape(MOD_ROWS, 1, N_MOD * d)
        w_in_bf = jnp.pad(w_in[layer], ((0, 0), (0, IN_W_PAD - IN_W))).astype(BF16)
        ret, swq, swkv, gqkv, gz, ab = _inproj_call(xcat, mod_l, norm_mix_w[layer], w_in_bf,
                                                    cos_t, sin_t, n_lat, b)
        o_rf, o_rb = _ret_call(ret, log_gamma[layer], n_lat)
        n_lat_blk = l // Q_BLOCK
        n_q_blk = n_lat_blk if last else n_lat_blk + lc // Q_BLOCK
        y_s = _swa_call(swq, swkv, swa_sinks[layer], n_lat_blk, n_q_blk)
        qkvn, gb = _gdn_prep_call(gqkv, ab, gdn_conv_w[layer], gdn_a_log[layer], gdn_dt_bias[layer],
                                  ones_head, n_lat)
        o_gf, o_gb = _gdn_call(qkvn, gb, consts, n_lat)
        xcat = _post_call(xcat, o_rf, o_rb, ret, y_s, o_gf, o_gb, gz, mod_l, norm_ffn_w[layer],
                          final_norm_w, ret_norm_w[layer], gdn_norm_w[layer], ones_head,
                          w_out[layer].astype(BF16), w_ffn_in[layer].astype(BF16),
                          w_ffn_out[layer].astype(BF16), n_lat, b,
                          n_lat if last else n_tiles, last)
    return xcat
```

```python
import functools

import numpy as np
import jax
import jax.numpy as jnp
from jax import lax
from jax.experimental import pallas as pl
from jax.experimental.pallas import tpu as pltpu

F32 = jnp.float32
BF16 = jnp.bfloat16

HEAD_DIM = 64
N_FREQ = HEAD_DIM // 4
GRID_W = 64
RET_HEADS = 4
SWA_Q_HEADS = 8
SWA_KV_HEADS = 2
GDN_HEADS = 4
RET_W = RET_HEADS * HEAD_DIM
SWA_W = SWA_Q_HEADS * HEAD_DIM
SWA_KV_W = SWA_KV_HEADS * HEAD_DIM
GDN_W = GDN_HEADS * HEAD_DIM
WINDOW = 128
Q_BLOCK = 128
GDN_CHUNK = 64
CONV_W = 3
ROPE_BASE = 10000.0
EPS = 1e-6
NEG_INF = -1e30
N_MOD = 6

LANES = 128
TILE = 256
MOD_ROWS = 16
VMEM_LIMIT = 56 * 1024 * 1024

COL_RET = 0
COL_SWA_Q = 4 * RET_W
COL_SWA_KV = COL_SWA_Q + SWA_W
COL_GDN_QKV = COL_SWA_KV + 2 * SWA_KV_W
COL_GDN_Z = COL_GDN_QKV + 3 * GDN_W
COL_AB = COL_GDN_Z + GDN_W
IN_W = COL_AB + 4 * GDN_HEADS
IN_W_PAD = COL_AB + LANES


def _dot(a, b):
    return jnp.dot(a, b, preferred_element_type=F32)


def _dot_nt(a, b):
    return lax.dot_general(a, b, (((1,), (1,)), ((), ())), preferred_element_type=F32)


def _split3(x):
    hi = x.astype(BF16)
    r = x - hi.astype(F32)
    mid = r.astype(BF16)
    lo = (r - mid.astype(F32)).astype(BF16)
    return hi, mid, lo


def _dot_x_sel(x, sel):
    hi, mid, lo = _split3(x)
    return _dot(hi, sel) + _dot(mid, sel) + _dot(lo, sel)


def _dot_sel_x(sel, x):
    hi, mid, lo = _split3(x)
    return _dot(sel, hi) + _dot(sel, mid) + _dot(sel, lo)


def _silu(x):
    return x * jax.nn.sigmoid(x)


def _head_masks(rows, width):
    lane = lax.broadcasted_iota(jnp.int32, (rows, width), 1)
    return [(lane // HEAD_DIM) == h for h in range(width // HEAD_DIM)]


def _head_sum(x, ones_bd):
    return _dot_x_sel(x, ones_bd)


def _cparams(n_axes):
    return pltpu.CompilerParams(dimension_semantics=("arbitrary",) * n_axes,
                                vmem_limit_bytes=VMEM_LIMIT)


def _ada_kernel(c_ref, w_ref, b_ref, o_ref):
    cs = _silu(c_ref[...])
    o_ref[0] = _dot(cs.astype(BF16), w_ref[0].astype(BF16)) + b_ref[0]


def _ada_call(c_pad, ada_w, ada_b):
    depth, d, n = ada_w.shape
    tn = 1536
    return pl.pallas_call(
        _ada_kernel,
        grid=(depth, n // tn),
        in_specs=[pl.BlockSpec((MOD_ROWS, d), lambda l, j: (0, 0)),
                  pl.BlockSpec((1, d, tn), lambda l, j: (l, 0, j)),
                  pl.BlockSpec((1, 1, tn), lambda l, j: (l, 0, j))],
        out_specs=pl.BlockSpec((1, MOD_ROWS, tn), lambda l, j: (l, 0, j)),
        out_shape=jax.ShapeDtypeStruct((depth, MOD_ROWS, n), F32),
        compiler_params=_cparams(2),
        name="ada_mod",
    )(c_pad, ada_w, ada_b.reshape(depth, 1, n))


def _rope(t, cos, sin_signed):
    lane = lax.broadcasted_iota(jnp.int32, t.shape, 1)
    first_half = (lane % (2 * N_FREQ)) < N_FREQ
    partner = jnp.where(first_half, pltpu.roll(t, LANES - N_FREQ, 1), pltpu.roll(t, N_FREQ, 1))
    return t * cos + partner * sin_signed


def _inproj_kernel(n_lat_tiles, x_ref, c_ref, sh_ref, sc_ref, nw_ref, w_ref, cos_ref, sin_ref,
                   ret_ref, swq_ref, swkv_ref, gqkv_ref, gz_ref, ab_ref):
    x = jnp.where(pl.program_id(1) < n_lat_tiles, x_ref[0], c_ref[0])
    y = x * lax.rsqrt(jnp.mean(x * x, axis=-1, keepdims=True) + EPS) * nw_ref[...]
    h = (y * (1.0 + sc_ref[0]) + sh_ref[0]).astype(BF16)
    cos = cos_ref[...]
    sin = sin_ref[...]

    def proj(col, width, out_ref, out_col, roped_lanes):
        r = _dot(h, w_ref[:, col:col + width])
        for g in range(width // LANES):
            rg = r[:, g * LANES:(g + 1) * LANES]
            if g * LANES < roped_lanes:
                rg = _rope(rg, cos, sin)
            out_ref[0, :, out_col + g * LANES:out_col + (g + 1) * LANES] = rg

    for g in range(4):
        proj(COL_RET + g * RET_W, RET_W, ret_ref, g * RET_W, RET_W if g < 2 else 0)
    for g in range(SWA_W // TILE):
        proj(COL_SWA_Q + g * TILE, TILE, swq_ref, g * TILE, TILE)
    proj(COL_SWA_KV, 2 * SWA_KV_W, swkv_ref, 0, SWA_KV_W)
    for g in range(3):
        proj(COL_GDN_QKV + g * GDN_W, GDN_W, gqkv_ref, g * GDN_W, 0)
    proj(COL_GDN_Z, GDN_W, gz_ref, 0, 0)
    proj(COL_AB, LANES, ab_ref, 0, 0)


def _row_source_specs(d, n_lat_tiles, ctx_tile):
    return [pl.BlockSpec((1, TILE, d), lambda bi, t: (bi, jnp.minimum(t, n_lat_tiles - 1), 0)),
            pl.BlockSpec((1, TILE, d), lambda bi, t: (bi, ctx_tile, 0))]


def _inproj_call(x_lat, x_ctx, ctx_tile, mod, norm_w, w_in_bf, cos_t, sin_t, n_lat_tiles, ctx_row):
    b, _, d = x_lat.shape
    nt = n_lat_tiles + 1
    tt = nt * TILE

    def mod_row(bi, t):
        return jnp.where(t >= n_lat_tiles, ctx_row, bi)

    widths = (4 * RET_W, SWA_W, 2 * SWA_KV_W, 3 * GDN_W, GDN_W, LANES)
    return pl.pallas_call(
        functools.partial(_inproj_kernel, n_lat_tiles),
        grid=(b, nt),
        in_specs=_row_source_specs(d, n_lat_tiles, ctx_tile) + [
                  pl.BlockSpec((1, 1, d), lambda bi, t: (mod_row(bi, t), 0, 0)),
                  pl.BlockSpec((1, 1, d), lambda bi, t: (mod_row(bi, t), 0, 1)),
                  pl.BlockSpec((1, d), lambda bi, t: (0, 0)),
                  pl.BlockSpec((d, IN_W_PAD), lambda bi, t: (0, 0)),
                  pl.BlockSpec((TILE, LANES), lambda bi, t: (t, 0)),
                  pl.BlockSpec((TILE, LANES), lambda bi, t: (t, 0))],
        out_specs=[pl.BlockSpec((1, TILE, w), lambda bi, t: (bi, t, 0)) for w in widths],
        out_shape=[jax.ShapeDtypeStruct((b, tt, w), F32) for w in widths],
        compiler_params=_cparams(2),
        name="in_proj",
    )(x_lat, x_ctx, mod, mod, norm_w.reshape(1, d), w_in_bf, cos_t, sin_t)


def _ret_kernel(lg_ref, lge_ref, rf_ref, rb_ref, of_ref, ob_ref, sf_ref, sb_ref):
    c = pl.program_id(1)

    @pl.when(c == 0)
    def _():
        sf_ref[...] = jnp.zeros_like(sf_ref)
        sb_ref[...] = jnp.zeros_like(sb_ref)

    masks = _head_masks(TILE, RET_W)
    ri = lax.broadcasted_iota(jnp.int32, (TILE, TILE), 0)
    ci = lax.broadcasted_iota(jnp.int32, (TILE, TILE), 1)
    bd = (ri // HEAD_DIM) == (ci // HEAD_DIM)
    idx = ri.astype(F32)

    def sweep(direction, r_ref, s_ref):
        blk = r_ref[0]
        q = blk[:, 0:RET_W] * (HEAD_DIM ** -0.5)
        k = blk[:, RET_W:2 * RET_W]
        v = blk[:, 2 * RET_W:3 * RET_W]
        lge = lge_ref[direction:direction + 1, :]
        if direction == 0:
            q_dec = jnp.exp(lge * (idx + 1.0))
            k_dec = jnp.exp(lge * (TILE - 1.0 - idx))
        else:
            q_dec = jnp.exp(lge * (TILE - idx))
            k_dec = jnp.exp(lge * idx)
        c_dec = jnp.exp(lge * float(TILE))
        s = s_ref[...]
        cross = _dot((q * q_dec).astype(BF16), s.astype(BF16))
        kt = (k * k_dec).T
        upd = _dot(kt.astype(BF16), v.astype(BF16))
        s_ref[...] = s * c_dec + jnp.where(bd, upd, 0.0)
        return cross, q, k, v

    o, q, k, v = sweep(0, rf_ref, sf_ref)
    diff = (ri - ci).astype(F32)
    kb = k.astype(BF16)
    for h in range(RET_HEADS):
        lgf = lg_ref[0, h]
        lgb = lg_ref[1, h]
        dm = (jnp.where(diff >= 0, jnp.exp(lgf * jnp.maximum(diff, 0.0)), 0.0)
              + jnp.where(diff <= 0, jnp.exp(lgb * jnp.maximum(-diff, 0.0)), 0.0))
        sc = _dot_nt(jnp.where(masks[h], q, 0.0).astype(BF16), kb)
        o = o + _dot((sc * dm).astype(BF16), jnp.where(masks[h], v, 0.0).astype(BF16))
    of_ref[0] = o
    ob_ref[0] = sweep(1, rb_ref, sb_ref)[0]


def _ret_call(ret, log_gamma, n_lat):
    b, tt, _ = ret.shape
    nt = tt // TILE
    lge = jnp.repeat(log_gamma, HEAD_DIM, axis=-1)

    def f_idx(bi, c):
        return (bi, jnp.where(c == 0, n_lat, c - 1), 0)

    def b_idx(bi, c):
        return (bi, jnp.where(c == 0, n_lat, n_lat - c), 0)

    return pl.pallas_call(
        _ret_kernel,
        grid=(b, nt),
        in_specs=[pl.BlockSpec(memory_space=pltpu.SMEM),
                  pl.BlockSpec((2, RET_W), lambda bi, c: (0, 0)),
                  pl.BlockSpec((1, TILE, 3 * RET_W), f_idx),
                  pl.BlockSpec((1, TILE, 3 * RET_W), b_idx)],
        out_specs=[pl.BlockSpec((1, TILE, RET_W), f_idx), pl.BlockSpec((1, TILE, RET_W), b_idx)],
        out_shape=[jax.ShapeDtypeStruct((b, tt, RET_W), F32)] * 2,
        scratch_shapes=[pltpu.VMEM((RET_W, RET_W), F32), pltpu.VMEM((RET_W, RET_W), F32)],
        compiler_params=_cparams(2),
        name="retention",
    )(log_gamma, lge, ret, ret)


SWA_BLOCKS_PER_STEP = 2


def _swa_kernel(n_lat_blk, sink_ref, q_ref, kv_ref, y_ref):
    for i in range(SWA_BLOCKS_PER_STEP):
        _swa_block(n_lat_blk, sink_ref, q_ref, kv_ref, y_ref,
                   pl.program_id(1) * SWA_BLOCKS_PER_STEP + i, slice(i * Q_BLOCK, (i + 1) * Q_BLOCK))


def _swa_block(n_lat_blk, sink_ref, q_ref, kv_ref, y_ref, j, q_rows_in_step):
    is_lat = j < n_lat_blk
    jc = jnp.minimum(j, n_lat_blk - 1)
    jp = jnp.maximum(jc - 1, 0)
    jn = jnp.minimum(jc + 1, n_lat_blk - 1)
    lat_rows = n_lat_blk * Q_BLOCK

    def piece(blk):
        return kv_ref[0, pl.ds(pl.multiple_of(blk * Q_BLOCK, Q_BLOCK), Q_BLOCK), :]

    kv_all = jnp.concatenate([piece(jp), piece(jc), piece(jn), kv_ref[0, lat_rows:, :]], axis=0)
    k_all = kv_all[:, 0:SWA_KV_W].astype(BF16)
    v_all = kv_all[:, SWA_KV_W:2 * SWA_KV_W].astype(BF16)
    n_piece = k_all.shape[0] // Q_BLOCK

    group = SWA_Q_HEADS // SWA_KV_HEADS
    rows = group * Q_BLOCK
    ql = lax.broadcasted_iota(jnp.int32, (rows, Q_BLOCK), 0) % Q_BLOCK
    kl = lax.broadcasted_iota(jnp.int32, (rows, Q_BLOCK), 1)
    ok_prev = (kl >= ql) & is_lat & (j >= 1)
    ok_next = (kl <= ql) & is_lat & (j <= n_lat_blk - 2)
    lane = lax.broadcasted_iota(jnp.int32, (Q_BLOCK, LANES), 1)
    low = lane < HEAD_DIM
    out_tiles = [[None, None] for _ in range(SWA_W // LANES)]

    for g in range(SWA_KV_HEADS):
        q_rows, sink_rows = [], []
        for hh in range(group):
            head = g * group + hh
            t, e = head // 2, head % 2
            qt = q_ref[0, q_rows_in_step, t * LANES:(t + 1) * LANES] * (HEAD_DIM ** -0.5)
            qm = jnp.where(low if e == 0 else ~low, qt, 0.0)
            if e != g:
                qm = pltpu.roll(qm, HEAD_DIM, 1)
            q_rows.append(qm)
            sink_rows.append(jnp.full((Q_BLOCK, 1), sink_ref[0, head], F32))
        qs = jnp.concatenate(q_rows, axis=0).astype(BF16)
        sink = jnp.concatenate(sink_rows, axis=0)
        s = _dot_nt(qs, k_all)

        def masked(i):
            s_i = s[:, i * Q_BLOCK:(i + 1) * Q_BLOCK]
            if i == 0:
                return jnp.where(ok_prev, s_i, NEG_INF)
            if i == 1:
                return jnp.where(is_lat, s_i, NEG_INF)
            if i == 2:
                return jnp.where(ok_next, s_i, NEG_INF)
            return s_i

        mx = masked(0)
        for i in range(1, n_piece):
            mx = jnp.maximum(mx, masked(i))
        m = jnp.maximum(jnp.max(mx, axis=-1, keepdims=True), sink)
        pr = [jnp.exp(masked(i) - m) for i in range(n_piece)]
        acc = pr[0]
        for p_i in pr[1:]:
            acc = acc + p_i
        den = jnp.sum(acc, axis=-1, keepdims=True) + jnp.exp(sink - m)
        pb = jnp.concatenate([p_i.astype(BF16) for p_i in pr], axis=1)
        o = _dot(pb, v_all) / den
        for hh in range(group):
            head = g * group + hh
            t, e = head // 2, head % 2
            oh = o[hh * Q_BLOCK:(hh + 1) * Q_BLOCK]
            if e != g:
                oh = pltpu.roll(oh, HEAD_DIM, 1)
            out_tiles[t][e] = oh
    for t in range(SWA_W // LANES):
        y_ref[0, q_rows_in_step, t * LANES:(t + 1) * LANES] = jnp.where(low, out_tiles[t][0], out_tiles[t][1])


def _swa_call(swq, swkv, sinks, n_lat_blk, n_q_blk):
    b, tt, _ = swq.shape
    step_rows = SWA_BLOCKS_PER_STEP * Q_BLOCK
    assert n_q_blk % SWA_BLOCKS_PER_STEP == 0
    return pl.pallas_call(
        functools.partial(_swa_kernel, n_lat_blk),
        grid=(b, n_q_blk // SWA_BLOCKS_PER_STEP),
        in_specs=[pl.BlockSpec(memory_space=pltpu.SMEM),
                  pl.BlockSpec((1, step_rows, SWA_W), lambda bi, j: (bi, j, 0)),
                  pl.BlockSpec((1, tt, 2 * SWA_KV_W), lambda bi, j: (bi, 0, 0))],
        out_specs=pl.BlockSpec((1, step_rows, SWA_W), lambda bi, j: (bi, j, 0)),
        out_shape=jax.ShapeDtypeStruct((b, n_q_blk * Q_BLOCK, SWA_W), F32),
        compiler_params=_cparams(2),
        name="swa",
    )(sinks.reshape(1, SWA_Q_HEADS), swq, swkv)


def _gdn_prep_kernel(n_lat, x_ref, hp_ref, hn_ref, ab_ref, cw_ref, alog_ref, dtb_ref, ones_ref,
                     qkv_ref, gb_ref):
    t = pl.program_id(1)
    x = x_ref[0]
    has_prev = ((t > 0) & (t < n_lat)).astype(F32)
    has_next = (t < n_lat - 1).astype(F32)
    row = lax.broadcasted_iota(jnp.int32, x.shape, 0)
    prev_row = hp_ref[0, 7:8, :] * has_prev
    next_row = hn_ref[0, 0:1, :] * has_next
    xm1 = jnp.where(row == 0, prev_row, pltpu.roll(x, 1, 0))
    xp1 = jnp.where(row == TILE - 1, next_row, pltpu.roll(x, TILE - 1, 0))
    conv = xm1 * cw_ref[0:1, :] + x * cw_ref[1:2, :] + xp1 * cw_ref[2:3, :]
    s = _silu(conv)
    ones_bd = ones_ref[...]

    def l2n(u):
        return u * lax.rsqrt(_head_sum(u * u, ones_bd) + EPS)

    qkv_ref[0, :, 0:GDN_W] = l2n(s[:, 0:GDN_W]) * (HEAD_DIM ** -0.5)
    qkv_ref[0, :, GDN_W:2 * GDN_W] = l2n(s[:, GDN_W:2 * GDN_W])
    qkv_ref[0, :, 2 * GDN_W:3 * GDN_W] = s[:, 2 * GDN_W:3 * GDN_W]

    ab = ab_ref[0]
    lane = lax.broadcasted_iota(jnp.int32, ab.shape, 1)
    g = -jnp.exp(alog_ref[...]) * jax.nn.softplus(ab + dtb_ref[...])
    beta = jax.nn.sigmoid(ab)
    gb_ref[0] = jnp.where(lane < 2 * GDN_HEADS, g, jnp.where(lane < 4 * GDN_HEADS, beta, 0.0))


def _gdn_prep_call(gqkv, ab, conv_w, a_log, dt_bias, ones_bd, n_lat):
    b, tt, w = gqkv.shape
    nt = tt // TILE
    sub = 8
    per = TILE // sub
    last = tt // sub - 1
    pad = LANES - 2 * GDN_HEADS
    alog_pad = jnp.pad(a_log.reshape(1, 2 * GDN_HEADS), ((0, 0), (0, pad)))
    dtb_pad = jnp.pad(dt_bias.reshape(1, 2 * GDN_HEADS), ((0, 0), (0, pad)))
    return pl.pallas_call(
        functools.partial(_gdn_prep_kernel, n_lat),
        grid=(b, nt),
        in_specs=[pl.BlockSpec((1, TILE, w), lambda bi, t: (bi, t, 0)),
                  pl.BlockSpec((1, sub, w), lambda bi, t: (bi, jnp.maximum(t * per - 1, 0), 0)),
                  pl.BlockSpec((1, sub, w), lambda bi, t: (bi, jnp.minimum((t + 1) * per, last), 0)),
                  pl.BlockSpec((1, TILE, LANES), lambda bi, t: (bi, t, 0)),
                  pl.BlockSpec((CONV_W, w), lambda bi, t: (0, 0)),
                  pl.BlockSpec((1, LANES), lambda bi, t: (0, 0)),
                  pl.BlockSpec((1, LANES), lambda bi, t: (0, 0)),
                  pl.BlockSpec((GDN_W, GDN_W), lambda bi, t: (0, 0))],
        out_specs=[pl.BlockSpec((1, TILE, w), lambda bi, t: (bi, t, 0)),
                   pl.BlockSpec((1, TILE, LANES), lambda bi, t: (bi, t, 0))],
        out_shape=[jax.ShapeDtypeStruct((b, tt, w), F32), jax.ShapeDtypeStruct((b, tt, LANES), F32)],
        compiler_params=_cparams(2),
        name="gdn_prep",
    )(gqkv, gqkv, gqkv, ab, conv_w, alog_pad, dtb_pad, ones_bd)


N_SUB = TILE // GDN_CHUNK


def _gdn_kernel(n_lat, ones_ref, tri_ref, xf_ref, gbf_ref, xb_ref, gbb_ref,
                of_ref, ob_ref, sf_ref, sb_ref):
    c = pl.program_id(1)

    @pl.when(c == 0)
    def _():
        sf_ref[...] = jnp.zeros_like(sf_ref)
        sb_ref[...] = jnp.zeros_like(sb_ref)

    ones_bd = ones_ref[...]
    masks = _head_masks(TILE, GDN_W)
    sub_masks = _head_masks(GDN_CHUNK, GDN_W)
    ri = lax.broadcasted_iota(jnp.int32, (TILE, TILE), 0)
    ci = lax.broadcasted_iota(jnp.int32, (TILE, TILE), 1)
    bd = (ri // GDN_CHUNK) == (ci // GDN_CHUNK)
    x_refs = (xf_ref, xb_ref)
    gb_refs = (gbf_ref, gbb_ref)
    s_refs = (sf_ref, sb_ref)
    o_refs = (of_ref, ob_ref)
    dirs = (0, 1)

    def expand(x, first_col):
        e = jnp.zeros((TILE, GDN_W), F32)
        for h in range(GDN_HEADS):
            e = jnp.where(masks[h], x[:, first_col + h:first_col + h + 1], e)
        return e

    pre = []
    for d in dirs:
        blk = x_refs[d][0]
        qn = blk[:, 0:GDN_W]
        kn = blk[:, GDN_W:2 * GDN_W]
        v = blk[:, 2 * GDN_W:3 * GDN_W]
        gbv = gb_refs[d][0]
        gc = _dot_sel_x(tri_ref[d], gbv)
        gtot = _dot_sel_x(ones_bd, gbv)
        gc_e = expand(gc, d * GDN_HEADS)
        gtot_e = expand(gtot, d * GDN_HEADS)
        beta_e = expand(gbv, 2 * GDN_HEADS + d * GDN_HEADS)
        egc = jnp.exp(gc_e)
        pre.append(dict(qn=qn, kn=kn, kn_b=kn.astype(BF16), gb_t=gbv.T, gc=gc, gc_t=gc.T,
                        vb_t=(v * beta_e).T, kb_t=(kn * beta_e * egc).T, qd=qn * egc,
                        kt=kn * jnp.exp(gtot_e - gc_e), gtot_dec=jnp.exp(gtot_e),
                        causal=bd & ((ri >= ci) if d == 0 else (ri <= ci)),
                        causal_t=bd & ((ri <= ci) if d == 0 else (ri >= ci))))

    chains = [(d, h) for h in range(GDN_HEADS) for d in dirs]
    m_t, z_acc, attn = {}, {}, {}
    for (d, h) in chains:
        p = pre[d]
        r = d * GDN_HEADS + h
        kk = _dot_nt(jnp.where(masks[h], p["kn"], 0.0).astype(BF16), p["kn_b"])
        qk = _dot_nt(jnp.where(masks[h], p["qn"], 0.0).astype(BF16), p["kn_b"])
        diff = p["gc"][:, r:r + 1] - p["gc_t"][r:r + 1, :]
        attn[(d, h)] = qk * jnp.exp(jnp.where(p["causal"], diff, NEG_INF))
        brow = p["gb_t"][2 * GDN_HEADS + r:2 * GDN_HEADS + r + 1, :]
        m_t[(d, h)] = jnp.where(ri == ci, 0.0,
                                kk * brow * jnp.exp(jnp.where(p["causal_t"], -diff, NEG_INF)))
        z_acc[(d, h)] = jnp.concatenate([p["vb_t"][h * HEAD_DIM:(h + 1) * HEAD_DIM],
                                         p["kb_t"][h * HEAD_DIM:(h + 1) * HEAD_DIM]], axis=0)

    def off_diag(s_blk):
        return ((ri // (2 * s_blk)) == (ci // (2 * s_blk))) & ((ri // s_blk) != (ci // s_blk))

    x_inv = {}
    for ch in chains:
        x_inv[ch] = (ri == ci).astype(F32) - jnp.where(off_diag(1), m_t[ch], 0.0)
    s_blk = 2
    while s_blk < GDN_CHUNK // 2:
        for ch in chains:
            xb = x_inv[ch].astype(BF16)
            pc = _dot(xb, jnp.where(off_diag(s_blk), m_t[ch], 0.0).astype(BF16))
            x_inv[ch] = x_inv[ch] - _dot(pc.astype(BF16), xb)
        s_blk *= 2
    for ch in chains:
        xb = x_inv[ch].astype(BF16)
        y = _dot(z_acc[ch].astype(BF16), xb)
        yc = _dot(y.astype(BF16), jnp.where(off_diag(s_blk), m_t[ch], 0.0).astype(BF16))
        z_acc[ch] = y - _dot(yc.astype(BF16), xb)

    u = [jnp.concatenate([z_acc[(d, h)][0:HEAD_DIM] for h in range(GDN_HEADS)], axis=0).T for d in dirs]
    w = [jnp.concatenate([z_acc[(d, h)][HEAD_DIM:2 * HEAD_DIM] for h in range(GDN_HEADS)], axis=0).T
         for d in dirs]

    state_bd = (ri // HEAD_DIM) == (ci // HEAD_DIM)
    kt_t = [pre[d]["kt"].T for d in dirs]
    s = [s_refs[d][...] for d in dirs]
    outs = [[None] * N_SUB for _ in dirs]
    for i in range(N_SUB):
        for d in dirs:
            sub = i if d == 0 else N_SUB - 1 - i
            r0 = sub * GDN_CHUNK
            wq = jnp.concatenate([w[d][r0:r0 + GDN_CHUNK], pre[d]["qd"][r0:r0 + GDN_CHUNK]], axis=0)
            ws = _dot(wq.astype(BF16), s[d].astype(BF16))
            v_new = u[d][r0:r0 + GDN_CHUNK] - ws[0:GDN_CHUNK]
            o_sub = ws[GDN_CHUNK:2 * GDN_CHUNK]
            for h in range(GDN_HEADS):
                a_blk = attn[(d, h)][r0:r0 + GDN_CHUNK, r0:r0 + GDN_CHUNK].astype(BF16)
                o_sub = o_sub + _dot(a_blk, jnp.where(sub_masks[h], v_new, 0.0).astype(BF16))
            outs[d][sub] = o_sub
            upd = _dot(kt_t[d][:, r0:r0 + GDN_CHUNK].astype(BF16), v_new.astype(BF16))
            s[d] = s[d] * pre[d]["gtot_dec"][r0:r0 + 1, :] + jnp.where(state_bd, upd, 0.0)
    for d in dirs:
        s_refs[d][...] = s[d]
        o_refs[d][0] = jnp.concatenate(outs[d], axis=0)


def _gdn_consts():
    i = np.arange(TILE)
    same = (i[:, None] // GDN_CHUNK) == (i[None, :] // GDN_CHUNK)
    tri = np.stack([same & (i[None, :] <= i[:, None]), same & (i[None, :] >= i[:, None])])
    return jnp.asarray(same, BF16), jnp.asarray(tri, BF16)


def _gdn_call(qkvn, gb, consts, n_lat):
    b, tt, w = qkvn.shape
    nt = tt // TILE
    ones_chunk, tri = consts

    def f_idx(bi, c):
        return (bi, jnp.where(c == 0, n_lat, c - 1), 0)

    def b_idx(bi, c):
        return (bi, jnp.where(c == 0, n_lat, n_lat - c), 0)

    return pl.pallas_call(
        functools.partial(_gdn_kernel, n_lat),
        grid=(b, nt),
        in_specs=[pl.BlockSpec((TILE, TILE), lambda bi, c: (0, 0)),
                  pl.BlockSpec((2, TILE, TILE), lambda bi, c: (0, 0, 0)),
                  pl.BlockSpec((1, TILE, w), f_idx),
                  pl.BlockSpec((1, TILE, LANES), f_idx),
                  pl.BlockSpec((1, TILE, w), b_idx),
                  pl.BlockSpec((1, TILE, LANES), b_idx)],
        out_specs=[pl.BlockSpec((1, TILE, GDN_W), f_idx), pl.BlockSpec((1, TILE, GDN_W), b_idx)],
        out_shape=[jax.ShapeDtypeStruct((b, tt, GDN_W), F32)] * 2,
        scratch_shapes=[pltpu.VMEM((GDN_W, GDN_W), F32), pltpu.VMEM((GDN_W, GDN_W), F32)],
        compiler_params=_cparams(2),
        name="gdn",
    )(ones_chunk, tri, qkvn, gb, qkvn, gb)


FF_CHUNK = 512
POST_TILE = 512


def _post_kernel(d_ff, final, n_lat_tiles, with_ctx, x_ref, *refs):
    if with_ctx:
        c_ref, refs = refs[0], refs[1:]
        x = jnp.where(pl.program_id(1) < n_lat_tiles, x_ref[0], c_ref[0])
    else:
        x = x_ref[0]
    (rf_ref, rb_ref, rg_ref, ys_ref, gf_ref, gb_ref, gz_ref, g1_ref, sh_ref, sc_ref, g2_ref, nw_ref,
     fw_ref, rnw_ref, gnw_ref, ones_ref, wo_ref, w1_ref, w2_ref, o_ref) = refs
    ones_head = ones_ref[...]

    def head_out(f_ref, b_ref, gate_ref, w_ref):
        o = f_ref[0] + b_ref[0]
        ms = _head_sum(o * o, ones_head) * (1.0 / HEAD_DIM)
        return (o * lax.rsqrt(ms + EPS) * w_ref[...] * _silu(gate_ref[0])).astype(BF16)

    mix = (_dot(head_out(rf_ref, rb_ref, rg_ref, rnw_ref), wo_ref[0:RET_W, :])
           + _dot(ys_ref[0].astype(BF16), wo_ref[RET_W:RET_W + SWA_W, :])
           + _dot(head_out(gf_ref, gb_ref, gz_ref, gnw_ref), wo_ref[RET_W + SWA_W:, :]))
    x1 = x + g1_ref[0] * mix
    y = x1 * lax.rsqrt(jnp.mean(x1 * x1, axis=-1, keepdims=True) + EPS) * nw_ref[...]
    h = (y * (1.0 + sc_ref[0]) + sh_ref[0]).astype(BF16)
    acc = jnp.zeros_like(x1)
    for c0 in range(0, d_ff, FF_CHUNK):
        cw = min(FF_CHUNK, d_ff - c0)
        gate = _dot(h, w1_ref[:, c0:c0 + cw])
        up = _dot(h, w1_ref[:, d_ff + c0:d_ff + c0 + cw])
        acc = acc + _dot((_silu(gate) * up).astype(BF16), w2_ref[c0:c0 + cw, :])
    x2 = x1 + g2_ref[0] * acc
    if final:
        x2 = x2 * lax.rsqrt(jnp.mean(x2 * x2, axis=-1, keepdims=True) + EPS) * fw_ref[...]
    o_ref[0] = x2


def _post_call(x_lat, x_ctx, ctx_tile, o_rf, o_rb, ret, y_s, o_gf, o_gb, gz, mod, norm_w, final_w,
               ret_norm_w, gdn_norm_w, ones_head, wo_bf, w1_bf, w2_bf, n_lat_tiles, ctx_row, final):
    b, _, d = x_lat.shape
    d_ff = w2_bf.shape[0]
    with_ctx = x_ctx is not None
    tm = TILE if with_ctx else POST_TILE
    n_steps = n_lat_tiles + 1 if with_ctx else n_lat_tiles * TILE // tm

    def mod_spec(k):
        if with_ctx:
            return pl.BlockSpec((1, 1, d), lambda bi, t: (jnp.where(t >= n_lat_tiles, ctx_row, bi), 0, k))
        return pl.BlockSpec((1, 1, d), lambda bi, t: (bi, 0, k))

    def tile_spec(w):
        return pl.BlockSpec((1, tm, w), lambda bi, t: (bi, t, 0))

    full = lambda shape: pl.BlockSpec(shape, lambda bi, t: (0,) * len(shape))
    x_specs = _row_source_specs(d, n_lat_tiles, ctx_tile) if with_ctx else [tile_spec(d)]
    x_args = (x_lat, x_ctx) if with_ctx else (x_lat,)
    return pl.pallas_call(
        functools.partial(_post_kernel, d_ff, final, n_lat_tiles, with_ctx),
        grid=(b, n_steps),
        in_specs=x_specs + [
                  tile_spec(RET_W), tile_spec(RET_W),
                  pl.BlockSpec((1, tm, RET_W), lambda bi, t: (bi, t, 3)),
                  tile_spec(SWA_W), tile_spec(GDN_W), tile_spec(GDN_W), tile_spec(GDN_W),
                  mod_spec(2), mod_spec(3), mod_spec(4), mod_spec(5),
                  full((1, d)), full((1, d)), full((1, RET_W)), full((1, GDN_W)), full(ones_head.shape),
                  full(wo_bf.shape), full(w1_bf.shape), full(w2_bf.shape)],
        out_specs=tile_spec(d),
        out_shape=jax.ShapeDtypeStruct((b, n_steps * tm, d), F32),
        compiler_params=_cparams(2),
        name="out_proj_ffn",
    )(*x_args, o_rf, o_rb, ret, y_s, o_gf, o_gb, gz, mod, mod, mod, mod, norm_w.reshape(1, d),
      final_w.reshape(1, d), ret_norm_w.reshape(1, RET_W),
      jnp.tile(gdn_norm_w, GDN_HEADS).reshape(1, GDN_W), ones_head, wo_bf, w1_bf, w2_bf)


def _rope_tables(n_lat_rows, n_ctx_rows):
    pos = np.arange(n_lat_rows)
    lane = np.arange(LANES) % HEAD_DIM
    axis = lane // (2 * N_FREQ)
    second_half = (lane % (2 * N_FREQ)) >= N_FREQ
    freqs = jnp.asarray(ROPE_BASE, F32) ** (-jnp.arange(N_FREQ, dtype=F32) / N_FREQ)
    coord = jnp.where(jnp.asarray(axis == 0)[None, :],
                      jnp.asarray(pos // GRID_W, F32)[:, None],
                      jnp.asarray(pos % GRID_W, F32)[:, None])
    ang = coord * freqs[jnp.asarray(lane % N_FREQ)][None, :]
    cos = jnp.cos(ang)
    sin = jnp.where(jnp.asarray(second_half)[None, :], jnp.sin(ang), -jnp.sin(ang))
    cos = jnp.concatenate([cos, jnp.ones((n_ctx_rows, LANES), F32)], axis=0)
    sin = jnp.concatenate([sin, jnp.zeros((n_ctx_rows, LANES), F32)], axis=0)
    return cos, sin


def kernel(x, c, ctx, c_ctx, ada_w, ada_b, norm_mix_w, norm_ffn_w, w_in, ret_rate, ret_norm_w,
           swa_sinks, gdn_conv_w, gdn_a_log, gdn_dt_bias, gdn_norm_w, w_out, w_ffn_in, w_ffn_out,
           final_norm_w):
    b, l, d = x.shape
    lc = ctx.shape[1]
    depth = ada_w.shape[0]
    assert l % POST_TILE == 0 and lc == TILE and b < MOD_ROWS and d % LANES == 0
    n_lat = l // TILE

    x_lat, x_ctx, ctx_tile = x, ctx, 0
    c_pad = jnp.zeros((MOD_ROWS, d), F32).at[:b].set(c).at[b].set(c_ctx)
    mod = _ada_call(c_pad, ada_w, ada_b)
    cos_t, sin_t = _rope_tables(l, lc)
    consts = _gdn_consts()
    i = np.arange(RET_W)
    ones_head = jnp.asarray((i[:, None] // HEAD_DIM) == (i[None, :] // HEAD_DIM), BF16)
    log_gamma = jnp.log1p(-jnp.exp2(-ret_rate.astype(F32)))

    for layer in range(depth):
        last = layer == depth - 1
        mod_l = mod[layer].reshape(MOD_ROWS, 1, N_MOD * d)
        w_in_bf = jnp.pad(w_in[layer], ((0, 0), (0, IN_W_PAD - IN_W))).astype(BF16)
        ret, swq, swkv, gqkv, gz, ab = _inproj_call(x_lat, x_ctx, ctx_tile, mod_l, norm_mix_w[layer],
                                                    w_in_bf, cos_t, sin_t, n_lat, b)
        o_rf, o_rb = _ret_call(ret, log_gamma[layer], n_lat)
        n_lat_blk = l // Q_BLOCK
        n_q_blk = n_lat_blk if last else n_lat_blk + lc // Q_BLOCK
        y_s = _swa_call(swq, swkv, swa_sinks[layer], n_lat_blk, n_q_blk)
        qkvn, gb = _gdn_prep_call(gqkv, ab, gdn_conv_w[layer], gdn_a_log[layer], gdn_dt_bias[layer],
                                  ones_head, n_lat)
        o_gf, o_gb = _gdn_call(qkvn, gb, consts, n_lat)
        x_new = _post_call(x_lat, None if last else x_ctx, ctx_tile, o_rf, o_rb, ret, y_s, o_gf, o_gb, gz,
                           mod_l, norm_ffn_w[layer], final_norm_w, ret_norm_w[layer], gdn_norm_w[layer],
                           ones_head, w_out[layer].astype(BF16), w_ffn_in[layer].astype(BF16),
                           w_ffn_out[layer].astype(BF16), n_lat, b, last)
        x_lat, x_ctx, ctx_tile = x_new, x_new, n_lat
    return x_lat
```

```python
import functools

import numpy as np
import jax
import jax.numpy as jnp
from jax import lax
from jax.experimental import pallas as pl
from jax.experimental.pallas import tpu as pltpu

F32 = jnp.float32
BF16 = jnp.bfloat16

HEAD_DIM = 64
N_FREQ = HEAD_DIM // 4
GRID_W = 64
RET_HEADS = 4
SWA_Q_HEADS = 8
SWA_KV_HEADS = 2
GDN_HEADS = 4
RET_W = RET_HEADS * HEAD_DIM
SWA_W = SWA_Q_HEADS * HEAD_DIM
SWA_KV_W = SWA_KV_HEADS * HEAD_DIM
GDN_W = GDN_HEADS * HEAD_DIM
WINDOW = 128
Q_BLOCK = 128
GDN_CHUNK = 64
CONV_W = 3
ROPE_BASE = 10000.0
EPS = 1e-6
NEG_INF = -1e30
N_MOD = 6

LANES = 128
TILE = 256
MOD_ROWS = 16
NB = 2
VMEM_LIMIT = 56 * 1024 * 1024

COL_RET = 0
COL_SWA_Q = 4 * RET_W
COL_SWA_KV = COL_SWA_Q + SWA_W
COL_GDN_QKV = COL_SWA_KV + 2 * SWA_KV_W
COL_GDN_Z = COL_GDN_QKV + 3 * GDN_W
COL_AB = COL_GDN_Z + GDN_W
IN_W = COL_AB + 4 * GDN_HEADS
IN_W_PAD = COL_AB + LANES


def _dot(a, b):
    return jnp.dot(a, b, preferred_element_type=F32)


def _dot_nt(a, b):
    return lax.dot_general(a, b, (((1,), (1,)), ((), ())), preferred_element_type=F32)


def _split3(x):
    hi = x.astype(BF16)
    r = x - hi.astype(F32)
    mid = r.astype(BF16)
    lo = (r - mid.astype(F32)).astype(BF16)
    return hi, mid, lo


def _dot_x_sel(x, sel):
    hi, mid, lo = _split3(x)
    return _dot(hi, sel) + _dot(mid, sel) + _dot(lo, sel)


def _dot_sel_x(sel, x):
    hi, mid, lo = _split3(x)
    return _dot(sel, hi) + _dot(sel, mid) + _dot(sel, lo)


def _silu(x):
    return x * jax.nn.sigmoid(x)


def _head_masks(rows, width):
    lane = lax.broadcasted_iota(jnp.int32, (rows, width), 1)
    return [(lane // HEAD_DIM) == h for h in range(width // HEAD_DIM)]


def _head_sum(x, ones_bd):
    return _dot_x_sel(x, ones_bd)


def _cparams(n_axes):
    return pltpu.CompilerParams(dimension_semantics=("arbitrary",) * n_axes,
                                vmem_limit_bytes=VMEM_LIMIT)


def _ada_kernel(c_ref, w_ref, b_ref, o_ref):
    cs = _silu(c_ref[...])
    o_ref[0] = _dot(cs.astype(BF16), w_ref[0].astype(BF16)) + b_ref[0]


def _ada_call(c_pad, ada_w, ada_b):
    depth, d, n = ada_w.shape
    tn = 1536
    return pl.pallas_call(
        _ada_kernel,
        grid=(depth, n // tn),
        in_specs=[pl.BlockSpec((MOD_ROWS, d), lambda l, j: (0, 0)),
                  pl.BlockSpec((1, d, tn), lambda l, j: (l, 0, j)),
                  pl.BlockSpec((1, 1, tn), lambda l, j: (l, 0, j))],
        out_specs=pl.BlockSpec((1, MOD_ROWS, tn), lambda l, j: (l, 0, j)),
        out_shape=jax.ShapeDtypeStruct((depth, MOD_ROWS, n), F32),
        compiler_params=_cparams(2),
        name="ada_mod",
    )(c_pad, ada_w, ada_b.reshape(depth, 1, n))


def _rope(t, cos, sin_signed):
    lane = lax.broadcasted_iota(jnp.int32, t.shape, 1)
    first_half = (lane % (2 * N_FREQ)) < N_FREQ
    partner = jnp.where(first_half, pltpu.roll(t, LANES - N_FREQ, 1), pltpu.roll(t, N_FREQ, 1))
    return t * cos + partner * sin_signed


def _inproj_kernel(n_lat_tiles, x_ref, c_ref, sh_ref, sc_ref, nw_ref, w_ref, cos_ref, sin_ref,
                   ret_ref, swq_ref, swkv_ref, gqkv_ref, gz_ref, ab_ref):
    is_lat = pl.program_id(1) < n_lat_tiles
    hs = []
    for bb in range(NB):
        x = jnp.where(is_lat, x_ref[bb], c_ref[bb])
        y = x * lax.rsqrt(jnp.mean(x * x, axis=-1, keepdims=True) + EPS) * nw_ref[...]
        hs.append((y * (1.0 + sc_ref[bb]) + sh_ref[bb]).astype(BF16))
    h = jnp.concatenate(hs, axis=0)
    cos = cos_ref[...]
    sin = sin_ref[...]

    def proj(col, width, out_ref, out_col, roped_lanes):
        r = _dot(h, w_ref[:, col:col + width])
        for bb in range(NB):
            for g in range(width // LANES):
                rg = r[bb * TILE:(bb + 1) * TILE, g * LANES:(g + 1) * LANES]
                if g * LANES < roped_lanes:
                    rg = _rope(rg, cos, sin)
                out_ref[bb, :, out_col + g * LANES:out_col + (g + 1) * LANES] = rg

    for g in range(4):
        proj(COL_RET + g * RET_W, RET_W, ret_ref, g * RET_W, RET_W if g < 2 else 0)
    for g in range(SWA_W // TILE):
        proj(COL_SWA_Q + g * TILE, TILE, swq_ref, g * TILE, TILE)
    proj(COL_SWA_KV, 2 * SWA_KV_W, swkv_ref, 0, SWA_KV_W)
    for g in range(3):
        proj(COL_GDN_QKV + g * GDN_W, GDN_W, gqkv_ref, g * GDN_W, 0)
    proj(COL_GDN_Z, GDN_W, gz_ref, 0, 0)
    proj(COL_AB, LANES, ab_ref, 0, 0)


def _row_source_specs(d, n_lat_tiles, ctx_tile):
    return [pl.BlockSpec((NB, TILE, d), lambda bi, t: (bi, jnp.minimum(t, n_lat_tiles - 1), 0)),
            pl.BlockSpec((NB, TILE, d), lambda bi, t: (bi, ctx_tile, 0))]


def _mod_spec(d, k, n_lat_tiles, ctx_row):
    if n_lat_tiles is None:
        return pl.BlockSpec((NB, 1, d), lambda bi, t: (bi, 0, k))
    return pl.BlockSpec((NB, 1, d), lambda bi, t: (jnp.where(t >= n_lat_tiles, ctx_row // NB, bi), 0, k))


def _inproj_call(x_lat, x_ctx, ctx_tile, mod, norm_w, w_in_bf, cos_t, sin_t, n_lat_tiles, ctx_row):
    b, _, d = x_lat.shape
    nt = n_lat_tiles + 1
    tt = nt * TILE
    widths = (4 * RET_W, SWA_W, 2 * SWA_KV_W, 3 * GDN_W, GDN_W, LANES)
    return pl.pallas_call(
        functools.partial(_inproj_kernel, n_lat_tiles),
        grid=(b // NB, nt),
        in_specs=_row_source_specs(d, n_lat_tiles, ctx_tile) + [
                  _mod_spec(d, 0, n_lat_tiles, ctx_row),
                  _mod_spec(d, 1, n_lat_tiles, ctx_row),
                  pl.BlockSpec((1, d), lambda bi, t: (0, 0)),
                  pl.BlockSpec((d, IN_W_PAD), lambda bi, t: (0, 0)),
                  pl.BlockSpec((TILE, LANES), lambda bi, t: (t, 0)),
                  pl.BlockSpec((TILE, LANES), lambda bi, t: (t, 0))],
        out_specs=[pl.BlockSpec((NB, TILE, w), lambda bi, t: (bi, t, 0)) for w in widths],
        out_shape=[jax.ShapeDtypeStruct((b, tt, w), F32) for w in widths],
        compiler_params=_cparams(2),
        name="in_proj",
    )(x_lat, x_ctx, mod, mod, norm_w.reshape(1, d), w_in_bf, cos_t, sin_t)


def _ret_kernel(lg_ref, lge_ref, rf_ref, rb_ref, of_ref, ob_ref, sf_ref, sb_ref):
    c = pl.program_id(1)

    @pl.when(c == 0)
    def _():
        sf_ref[...] = jnp.zeros_like(sf_ref)
        sb_ref[...] = jnp.zeros_like(sb_ref)

    masks = _head_masks(TILE, RET_W)
    ri = lax.broadcasted_iota(jnp.int32, (TILE, TILE), 0)
    ci = lax.broadcasted_iota(jnp.int32, (TILE, TILE), 1)
    bd = (ri // HEAD_DIM) == (ci // HEAD_DIM)
    idx = ri.astype(F32)
    diff = (ri - ci).astype(F32)
    dm = [jnp.where(diff >= 0, jnp.exp(lg_ref[0, h] * jnp.maximum(diff, 0.0)), 0.0)
          + jnp.where(diff <= 0, jnp.exp(lg_ref[1, h] * jnp.maximum(-diff, 0.0)), 0.0)
          for h in range(RET_HEADS)]
    dec = []
    for direction in (0, 1):
        lge = lge_ref[direction:direction + 1, :]
        if direction == 0:
            dec.append((jnp.exp(lge * (idx + 1.0)), jnp.exp(lge * (TILE - 1.0 - idx)),
                        jnp.exp(lge * float(TILE))))
        else:
            dec.append((jnp.exp(lge * (TILE - idx)), jnp.exp(lge * idx), jnp.exp(lge * float(TILE))))

    def sweep(direction, bb, r_ref, s_ref):
        blk = r_ref[bb]
        q = blk[:, 0:RET_W] * (HEAD_DIM ** -0.5)
        k = blk[:, RET_W:2 * RET_W]
        v = blk[:, 2 * RET_W:3 * RET_W]
        q_dec, k_dec, c_dec = dec[direction]
        s = s_ref[bb]
        cross = _dot((q * q_dec).astype(BF16), s.astype(BF16))
        kt = (k * k_dec).T
        upd = _dot(kt.astype(BF16), v.astype(BF16))
        s_ref[bb] = s * c_dec + jnp.where(bd, upd, 0.0)
        return cross, q, k, v

    for bb in range(NB):
        o, q, k, v = sweep(0, bb, rf_ref, sf_ref)
        kb = k.astype(BF16)
        for h in range(RET_HEADS):
            sc = _dot_nt(jnp.where(masks[h], q, 0.0).astype(BF16), kb)
            o = o + _dot((sc * dm[h]).astype(BF16), jnp.where(masks[h], v, 0.0).astype(BF16))
        of_ref[bb] = o
        ob_ref[bb] = sweep(1, bb, rb_ref, sb_ref)[0]


def _ret_call(ret, log_gamma, n_lat):
    b, tt, _ = ret.shape
    nt = tt // TILE
    lge = jnp.repeat(log_gamma, HEAD_DIM, axis=-1)

    def f_idx(bi, c):
        return (bi, jnp.where(c == 0, n_lat, c - 1), 0)

    def b_idx(bi, c):
        return (bi, jnp.where(c == 0, n_lat, n_lat - c), 0)

    return pl.pallas_call(
        _ret_kernel,
        grid=(b // NB, nt),
        in_specs=[pl.BlockSpec(memory_space=pltpu.SMEM),
                  pl.BlockSpec((2, RET_W), lambda bi, c: (0, 0)),
                  pl.BlockSpec((NB, TILE, 3 * RET_W), f_idx),
                  pl.BlockSpec((NB, TILE, 3 * RET_W), b_idx)],
        out_specs=[pl.BlockSpec((NB, TILE, RET_W), f_idx), pl.BlockSpec((NB, TILE, RET_W), b_idx)],
        out_shape=[jax.ShapeDtypeStruct((b, tt, RET_W), F32)] * 2,
        scratch_shapes=[pltpu.VMEM((NB, RET_W, RET_W), F32), pltpu.VMEM((NB, RET_W, RET_W), F32)],
        compiler_params=_cparams(2),
        name="retention",
    )(log_gamma, lge, ret, ret)


SWA_BLOCKS_PER_STEP = 2


def _swa_kernel(n_lat_blk, sink_ref, q_ref, kv_ref, y_ref):
    for i in range(SWA_BLOCKS_PER_STEP):
        _swa_block(n_lat_blk, sink_ref, q_ref, kv_ref, y_ref,
                   pl.program_id(1) * SWA_BLOCKS_PER_STEP + i, slice(i * Q_BLOCK, (i + 1) * Q_BLOCK))


def _swa_block(n_lat_blk, sink_ref, q_ref, kv_ref, y_ref, j, q_rows_in_step):
    is_lat = j < n_lat_blk
    jc = jnp.minimum(j, n_lat_blk - 1)
    jp = jnp.maximum(jc - 1, 0)
    jn = jnp.minimum(jc + 1, n_lat_blk - 1)
    lat_rows = n_lat_blk * Q_BLOCK

    def piece(blk):
        return kv_ref[0, pl.ds(pl.multiple_of(blk * Q_BLOCK, Q_BLOCK), Q_BLOCK), :]

    kv_all = jnp.concatenate([piece(jp), piece(jc), piece(jn), kv_ref[0, lat_rows:, :]], axis=0)
    k_all = kv_all[:, 0:SWA_KV_W].astype(BF16)
    v_all = kv_all[:, SWA_KV_W:2 * SWA_KV_W].astype(BF16)
    n_piece = k_all.shape[0] // Q_BLOCK

    group = SWA_Q_HEADS // SWA_KV_HEADS
    rows = group * Q_BLOCK
    ql = lax.broadcasted_iota(jnp.int32, (rows, Q_BLOCK), 0) % Q_BLOCK
    kl = lax.broadcasted_iota(jnp.int32, (rows, Q_BLOCK), 1)
    ok_prev = (kl >= ql) & is_lat & (j >= 1)
    ok_next = (kl <= ql) & is_lat & (j <= n_lat_blk - 2)
    lane = lax.broadcasted_iota(jnp.int32, (Q_BLOCK, LANES), 1)
    low = lane < HEAD_DIM
    out_tiles = [[None, None] for _ in range(SWA_W // LANES)]

    for g in range(SWA_KV_HEADS):
        q_rows, sink_rows = [], []
        for hh in range(group):
            head = g * group + hh
            t, e = head // 2, head % 2
            qt = q_ref[0, q_rows_in_step, t * LANES:(t + 1) * LANES] * (HEAD_DIM ** -0.5)
            qm = jnp.where(low if e == 0 else ~low, qt, 0.0)
            if e != g:
                qm = pltpu.roll(qm, HEAD_DIM, 1)
            q_rows.append(qm)
            sink_rows.append(jnp.full((Q_BLOCK, 1), sink_ref[0, head], F32))
        qs = jnp.concatenate(q_rows, axis=0).astype(BF16)
        sink = jnp.concatenate(sink_rows, axis=0)
        s = _dot_nt(qs, k_all)

        def masked(i):
            s_i = s[:, i * Q_BLOCK:(i + 1) * Q_BLOCK]
            if i == 0:
                return jnp.where(ok_prev, s_i, NEG_INF)
            if i == 1:
                return jnp.where(is_lat, s_i, NEG_INF)
            if i == 2:
                return jnp.where(ok_next, s_i, NEG_INF)
            return s_i

        mx = masked(0)
        for i in range(1, n_piece):
            mx = jnp.maximum(mx, masked(i))
        m = jnp.maximum(jnp.max(mx, axis=-1, keepdims=True), sink)
        pr = [jnp.exp(masked(i) - m) for i in range(n_piece)]
        acc = pr[0]
        for p_i in pr[1:]:
            acc = acc + p_i
        den = jnp.sum(acc, axis=-1, keepdims=True) + jnp.exp(sink - m)
        pb = jnp.concatenate([p_i.astype(BF16) for p_i in pr], axis=1)
        o = _dot(pb, v_all) / den
        for hh in range(group):
            head = g * group + hh
            t, e = head // 2, head % 2
            oh = o[hh * Q_BLOCK:(hh + 1) * Q_BLOCK]
            if e != g:
                oh = pltpu.roll(oh, HEAD_DIM, 1)
            out_tiles[t][e] = oh
    for t in range(SWA_W // LANES):
        y_ref[0, q_rows_in_step, t * LANES:(t + 1) * LANES] = jnp.where(low, out_tiles[t][0], out_tiles[t][1])


def _swa_call(swq, swkv, sinks, n_lat_blk, n_q_blk):
    b, tt, _ = swq.shape
    step_rows = SWA_BLOCKS_PER_STEP * Q_BLOCK
    assert n_q_blk % SWA_BLOCKS_PER_STEP == 0
    return pl.pallas_call(
        functools.partial(_swa_kernel, n_lat_blk),
        grid=(b, n_q_blk // SWA_BLOCKS_PER_STEP),
        in_specs=[pl.BlockSpec(memory_space=pltpu.SMEM),
                  pl.BlockSpec((1, step_rows, SWA_W), lambda bi, j: (bi, j, 0)),
                  pl.BlockSpec((1, tt, 2 * SWA_KV_W), lambda bi, j: (bi, 0, 0))],
        out_specs=pl.BlockSpec((1, step_rows, SWA_W), lambda bi, j: (bi, j, 0)),
        out_shape=jax.ShapeDtypeStruct((b, n_q_blk * Q_BLOCK, SWA_W), F32),
        compiler_params=_cparams(2),
        name="swa",
    )(sinks.reshape(1, SWA_Q_HEADS), swq, swkv)


def _gdn_prep_kernel(n_lat, x_ref, hp_ref, hn_ref, ab_ref, cw_ref, alog_ref, dtb_ref, ones_ref,
                     qkv_ref, gb_ref):
    t = pl.program_id(1)
    has_prev = ((t > 0) & (t < n_lat)).astype(F32)
    has_next = (t < n_lat - 1).astype(F32)
    ones_bd = ones_ref[...]

    def l2n(u):
        return u * lax.rsqrt(_head_sum(u * u, ones_bd) + EPS)

    for bb in range(NB):
        x = x_ref[bb]
        row = lax.broadcasted_iota(jnp.int32, x.shape, 0)
        prev_row = hp_ref[bb, 7:8, :] * has_prev
        next_row = hn_ref[bb, 0:1, :] * has_next
        xm1 = jnp.where(row == 0, prev_row, pltpu.roll(x, 1, 0))
        xp1 = jnp.where(row == TILE - 1, next_row, pltpu.roll(x, TILE - 1, 0))
        conv = xm1 * cw_ref[0:1, :] + x * cw_ref[1:2, :] + xp1 * cw_ref[2:3, :]
        s = _silu(conv)
        qkv_ref[bb, :, 0:GDN_W] = l2n(s[:, 0:GDN_W]) * (HEAD_DIM ** -0.5)
        qkv_ref[bb, :, GDN_W:2 * GDN_W] = l2n(s[:, GDN_W:2 * GDN_W])
        qkv_ref[bb, :, 2 * GDN_W:3 * GDN_W] = s[:, 2 * GDN_W:3 * GDN_W]

        ab = ab_ref[bb]
        lane = lax.broadcasted_iota(jnp.int32, ab.shape, 1)
        g = -jnp.exp(alog_ref[...]) * jax.nn.softplus(ab + dtb_ref[...])
        beta = jax.nn.sigmoid(ab)
        gb_ref[bb] = jnp.where(lane < 2 * GDN_HEADS, g, jnp.where(lane < 4 * GDN_HEADS, beta, 0.0))


def _gdn_prep_call(gqkv, ab, conv_w, a_log, dt_bias, ones_bd, n_lat):
    b, tt, w = gqkv.shape
    nt = tt // TILE
    sub = 8
    per = TILE // sub
    last = tt // sub - 1
    pad = LANES - 2 * GDN_HEADS
    alog_pad = jnp.pad(a_log.reshape(1, 2 * GDN_HEADS), ((0, 0), (0, pad)))
    dtb_pad = jnp.pad(dt_bias.reshape(1, 2 * GDN_HEADS), ((0, 0), (0, pad)))
    return pl.pallas_call(
        functools.partial(_gdn_prep_kernel, n_lat),
        grid=(b // NB, nt),
        in_specs=[pl.BlockSpec((NB, TILE, w), lambda bi, t: (bi, t, 0)),
                  pl.BlockSpec((NB, sub, w), lambda bi, t: (bi, jnp.maximum(t * per - 1, 0), 0)),
                  pl.BlockSpec((NB, sub, w), lambda bi, t: (bi, jnp.minimum((t + 1) * per, last), 0)),
                  pl.BlockSpec((NB, TILE, LANES), lambda bi, t: (bi, t, 0)),
                  pl.BlockSpec((CONV_W, w), lambda bi, t: (0, 0)),
                  pl.BlockSpec((1, LANES), lambda bi, t: (0, 0)),
                  pl.BlockSpec((1, LANES), lambda bi, t: (0, 0)),
                  pl.BlockSpec((GDN_W, GDN_W), lambda bi, t: (0, 0))],
        out_specs=[pl.BlockSpec((NB, TILE, w), lambda bi, t: (bi, t, 0)),
                   pl.BlockSpec((NB, TILE, LANES), lambda bi, t: (bi, t, 0))],
        out_shape=[jax.ShapeDtypeStruct((b, tt, w), F32), jax.ShapeDtypeStruct((b, tt, LANES), F32)],
        compiler_params=_cparams(2),
        name="gdn_prep",
    )(gqkv, gqkv, gqkv, ab, conv_w, alog_pad, dtb_pad, ones_bd)


N_SUB = TILE // GDN_CHUNK


def _gdn_kernel(n_lat, ones_ref, tri_ref, xf_ref, gbf_ref, xb_ref, gbb_ref,
                of_ref, ob_ref, sf_ref, sb_ref):
    c = pl.program_id(1)

    @pl.when(c == 0)
    def _():
        sf_ref[...] = jnp.zeros_like(sf_ref)
        sb_ref[...] = jnp.zeros_like(sb_ref)

    ones_bd = ones_ref[...]
    masks = _head_masks(TILE, GDN_W)
    sub_masks = _head_masks(GDN_CHUNK, GDN_W)
    ri = lax.broadcasted_iota(jnp.int32, (TILE, TILE), 0)
    ci = lax.broadcasted_iota(jnp.int32, (TILE, TILE), 1)
    bd = (ri // GDN_CHUNK) == (ci // GDN_CHUNK)
    x_refs = (xf_ref, xb_ref)
    gb_refs = (gbf_ref, gbb_ref)
    s_refs = (sf_ref, sb_ref)
    o_refs = (of_ref, ob_ref)
    dirs = tuple(range(2 * NB))

    def expand(x, first_col):
        e = jnp.zeros((TILE, GDN_W), F32)
        for h in range(GDN_HEADS):
            e = jnp.where(masks[h], x[:, first_col + h:first_col + h + 1], e)
        return e

    pre = []
    for d in dirs:
        blk = x_refs[d % 2][d // 2]
        qn = blk[:, 0:GDN_W]
        kn = blk[:, GDN_W:2 * GDN_W]
        v = blk[:, 2 * GDN_W:3 * GDN_W]
        gbv = gb_refs[d % 2][d // 2]
        gc = _dot_sel_x(tri_ref[d % 2], gbv)
        gtot = _dot_sel_x(ones_bd, gbv)
        gc_e = expand(gc, (d % 2) * GDN_HEADS)
        gtot_e = expand(gtot, (d % 2) * GDN_HEADS)
        beta_e = expand(gbv, 2 * GDN_HEADS + (d % 2) * GDN_HEADS)
        egc = jnp.exp(gc_e)
        pre.append(dict(qn=qn, kn=kn, kn_b=kn.astype(BF16), gb_t=gbv.T, gc=gc, gc_t=gc.T,
                        vb_t=(v * beta_e).T, kb_t=(kn * beta_e * egc).T, qd=qn * egc,
                        kt=kn * jnp.exp(gtot_e - gc_e), gtot_dec=jnp.exp(gtot_e),
                        causal=bd & ((ri >= ci) if d % 2 == 0 else (ri <= ci)),
                        causal_t=bd & ((ri <= ci) if d % 2 == 0 else (ri >= ci))))

    chains = [(d, h) for h in range(GDN_HEADS) for d in dirs]
    m_t, z_acc, attn = {}, {}, {}
    for (d, h) in chains:
        p = pre[d]
        r = (d % 2) * GDN_HEADS + h
        kk = _dot_nt(jnp.where(masks[h], p["kn"], 0.0).astype(BF16), p["kn_b"])
        qk = _dot_nt(jnp.where(masks[h], p["qn"], 0.0).astype(BF16), p["kn_b"])
        diff = p["gc"][:, r:r + 1] - p["gc_t"][r:r + 1, :]
        attn[(d, h)] = qk * jnp.exp(jnp.where(p["causal"], diff, NEG_INF))
        brow = p["gb_t"][2 * GDN_HEADS + r:2 * GDN_HEADS + r + 1, :]
        m_t[(d, h)] = jnp.where(ri == ci, 0.0,
                                kk * brow * jnp.exp(jnp.where(p["causal_t"], -diff, NEG_INF)))
        z_acc[(d, h)] = jnp.concatenate([p["vb_t"][h * HEAD_DIM:(h + 1) * HEAD_DIM],
                                         p["kb_t"][h * HEAD_DIM:(h + 1) * HEAD_DIM]], axis=0)

    def off_diag(s_blk):
        return ((ri // (2 * s_blk)) == (ci // (2 * s_blk))) & ((ri // s_blk) != (ci // s_blk))

    x_inv = {}
    for ch in chains:
        x_inv[ch] = (ri == ci).astype(F32) - jnp.where(off_diag(1), m_t[ch], 0.0)
    s_blk = 2
    while s_blk < GDN_CHUNK // 2:
        for ch in chains:
            xb = x_inv[ch].astype(BF16)
            pc = _dot(xb, jnp.where(off_diag(s_blk), m_t[ch], 0.0).astype(BF16))
            x_inv[ch] = x_inv[ch] - _dot(pc.astype(BF16), xb)
        s_blk *= 2
    for ch in chains:
        xb = x_inv[ch].astype(BF16)
        y = _dot(z_acc[ch].astype(BF16), xb)
        yc = _dot(y.astype(BF16), jnp.where(off_diag(s_blk), m_t[ch], 0.0).astype(BF16))
        z_acc[ch] = y - _dot(yc.astype(BF16), xb)

    u = [jnp.concatenate([z_acc[(d, h)][0:HEAD_DIM] for h in range(GDN_HEADS)], axis=0).T for d in dirs]
    w = [jnp.concatenate([z_acc[(d, h)][HEAD_DIM:2 * HEAD_DIM] for h in range(GDN_HEADS)], axis=0).T
         for d in dirs]

    state_bd = (ri // HEAD_DIM) == (ci // HEAD_DIM)
    kt_t = [pre[d]["kt"].T for d in dirs]
    s = [s_refs[d % 2][d // 2] for d in dirs]
    outs = [[None] * N_SUB for _ in dirs]
    for i in range(N_SUB):
        for d in dirs:
            sub = i if d % 2 == 0 else N_SUB - 1 - i
            r0 = sub * GDN_CHUNK
            wq = jnp.concatenate([w[d][r0:r0 + GDN_CHUNK], pre[d]["qd"][r0:r0 + GDN_CHUNK]], axis=0)
            ws = _dot(wq.astype(BF16), s[d].astype(BF16))
            v_new = u[d][r0:r0 + GDN_CHUNK] - ws[0:GDN_CHUNK]
            o_sub = ws[GDN_CHUNK:2 * GDN_CHUNK]
            for h in range(GDN_HEADS):
                a_blk = attn[(d, h)][r0:r0 + GDN_CHUNK, r0:r0 + GDN_CHUNK].astype(BF16)
                o_sub = o_sub + _dot(a_blk, jnp.where(sub_masks[h], v_new, 0.0).astype(BF16))
            outs[d][sub] = o_sub
            upd = _dot(kt_t[d][:, r0:r0 + GDN_CHUNK].astype(BF16), v_new.astype(BF16))
            s[d] = s[d] * pre[d]["gtot_dec"][r0:r0 + 1, :] + jnp.where(state_bd, upd, 0.0)
    for d in dirs:
        s_refs[d % 2][d // 2] = s[d]
        o_refs[d % 2][d // 2] = jnp.concatenate(outs[d], axis=0)


def _gdn_consts():
    i = np.arange(TILE)
    same = (i[:, None] // GDN_CHUNK) == (i[None, :] // GDN_CHUNK)
    tri = np.stack([same & (i[None, :] <= i[:, None]), same & (i[None, :] >= i[:, None])])
    return jnp.asarray(same, BF16), jnp.asarray(tri, BF16)


def _gdn_call(qkvn, gb, consts, n_lat):
    b, tt, w = qkvn.shape
    nt = tt // TILE
    ones_chunk, tri = consts

    def f_idx(bi, c):
        return (bi, jnp.where(c == 0, n_lat, c - 1), 0)

    def b_idx(bi, c):
        return (bi, jnp.where(c == 0, n_lat, n_lat - c), 0)

    return pl.pallas_call(
        functools.partial(_gdn_kernel, n_lat),
        grid=(b // NB, nt),
        in_specs=[pl.BlockSpec((TILE, TILE), lambda bi, c: (0, 0)),
                  pl.BlockSpec((2, TILE, TILE), lambda bi, c: (0, 0, 0)),
                  pl.BlockSpec((NB, TILE, w), f_idx),
                  pl.BlockSpec((NB, TILE, LANES), f_idx),
                  pl.BlockSpec((NB, TILE, w), b_idx),
                  pl.BlockSpec((NB, TILE, LANES), b_idx)],
        out_specs=[pl.BlockSpec((NB, TILE, GDN_W), f_idx), pl.BlockSpec((NB, TILE, GDN_W), b_idx)],
        out_shape=[jax.ShapeDtypeStruct((b, tt, GDN_W), F32)] * 2,
        scratch_shapes=[pltpu.VMEM((NB, GDN_W, GDN_W), F32), pltpu.VMEM((NB, GDN_W, GDN_W), F32)],
        compiler_params=_cparams(2),
        name="gdn",
    )(ones_chunk, tri, qkvn, gb, qkvn, gb)


FF_CHUNK = 512


def _post_kernel(d_ff, final, n_lat_tiles, with_ctx, x_ref, *refs):
    if with_ctx:
        c_ref, refs = refs[0], refs[1:]
        is_lat = pl.program_id(1) < n_lat_tiles
        xs = [jnp.where(is_lat, x_ref[bb], c_ref[bb]) for bb in range(NB)]
    else:
        xs = [x_ref[bb] for bb in range(NB)]
    (rf_ref, rb_ref, rg_ref, ys_ref, gf_ref, gb_ref, gz_ref, g1_ref, sh_ref, sc_ref, g2_ref, nw_ref,
     fw_ref, rnw_ref, gnw_ref, ones_ref, wo_ref, w1_ref, w2_ref, o_ref) = refs
    ones_head = ones_ref[...]

    def rows(per_sample):
        return jnp.concatenate([per_sample(bb) for bb in range(NB)], axis=0)

    def head_out(f_ref, b_ref, gate_ref, w_ref):
        def one(bb):
            o = f_ref[bb] + b_ref[bb]
            ms = _head_sum(o * o, ones_head) * (1.0 / HEAD_DIM)
            return (o * lax.rsqrt(ms + EPS) * w_ref[...] * _silu(gate_ref[bb])).astype(BF16)
        return rows(one)

    mix = (_dot(head_out(rf_ref, rb_ref, rg_ref, rnw_ref), wo_ref[0:RET_W, :])
           + _dot(rows(lambda bb: ys_ref[bb].astype(BF16)), wo_ref[RET_W:RET_W + SWA_W, :])
           + _dot(head_out(gf_ref, gb_ref, gz_ref, gnw_ref), wo_ref[RET_W + SWA_W:, :]))
    x1 = [xs[bb] + g1_ref[bb] * mix[bb * TILE:(bb + 1) * TILE] for bb in range(NB)]

    def ffn_in(bb):
        y = x1[bb] * lax.rsqrt(jnp.mean(x1[bb] * x1[bb], axis=-1, keepdims=True) + EPS) * nw_ref[...]
        return (y * (1.0 + sc_ref[bb]) + sh_ref[bb]).astype(BF16)

    h = rows(ffn_in)
    acc = jnp.zeros((NB * TILE, x1[0].shape[1]), F32)
    for c0 in range(0, d_ff, FF_CHUNK):
        cw = min(FF_CHUNK, d_ff - c0)
        gate = _dot(h, w1_ref[:, c0:c0 + cw])
        up = _dot(h, w1_ref[:, d_ff + c0:d_ff + c0 + cw])
        acc = acc + _dot((_silu(gate) * up).astype(BF16), w2_ref[c0:c0 + cw, :])
    for bb in range(NB):
        x2 = x1[bb] + g2_ref[bb] * acc[bb * TILE:(bb + 1) * TILE]
        if final:
            x2 = x2 * lax.rsqrt(jnp.mean(x2 * x2, axis=-1, keepdims=True) + EPS) * fw_ref[...]
        o_ref[bb] = x2


def _post_call(x_lat, x_ctx, ctx_tile, o_rf, o_rb, ret, y_s, o_gf, o_gb, gz, mod, norm_w, final_w,
               ret_norm_w, gdn_norm_w, ones_head, wo_bf, w1_bf, w2_bf, n_lat_tiles, ctx_row, final):
    b, _, d = x_lat.shape
    d_ff = w2_bf.shape[0]
    with_ctx = x_ctx is not None
    n_steps = n_lat_tiles + 1 if with_ctx else n_lat_tiles

    def mod_spec(k):
        return _mod_spec(d, k, n_lat_tiles if with_ctx else None, ctx_row)

    def tile_spec(w):
        return pl.BlockSpec((NB, TILE, w), lambda bi, t: (bi, t, 0))

    full = lambda shape: pl.BlockSpec(shape, lambda bi, t: (0,) * len(shape))
    x_specs = _row_source_specs(d, n_lat_tiles, ctx_tile) if with_ctx else [tile_spec(d)]
    x_args = (x_lat, x_ctx) if with_ctx else (x_lat,)
    return pl.pallas_call(
        functools.partial(_post_kernel, d_ff, final, n_lat_tiles, with_ctx),
        grid=(b // NB, n_steps),
        in_specs=x_specs + [
                  tile_spec(RET_W), tile_spec(RET_W),
                  pl.BlockSpec((NB, TILE, RET_W), lambda bi, t: (bi, t, 3)),
                  tile_spec(SWA_W), tile_spec(GDN_W), tile_spec(GDN_W), tile_spec(GDN_W),
                  mod_spec(2), mod_spec(3), mod_spec(4), mod_spec(5),
                  full((1, d)), full((1, d)), full((1, RET_W)), full((1, GDN_W)), full(ones_head.shape),
                  full(wo_bf.shape), full(w1_bf.shape), full(w2_bf.shape)],
        out_specs=tile_spec(d),
        out_shape=jax.ShapeDtypeStruct((b, n_steps * TILE, d), F32),
        compiler_params=_cparams(2),
        name="out_proj_ffn",
    )(*x_args, o_rf, o_rb, ret, y_s, o_gf, o_gb, gz, mod, mod, mod, mod, norm_w.reshape(1, d),
      final_w.reshape(1, d), ret_norm_w.reshape(1, RET_W),
      jnp.tile(gdn_norm_w, GDN_HEADS).reshape(1, GDN_W), ones_head, wo_bf, w1_bf, w2_bf)


def _rope_tables(n_lat_rows, n_ctx_rows):
    pos = np.arange(n_lat_rows)
    lane = np.arange(LANES) % HEAD_DIM
    axis = lane // (2 * N_FREQ)
    second_half = (lane % (2 * N_FREQ)) >= N_FREQ
    freqs = jnp.asarray(ROPE_BASE, F32) ** (-jnp.arange(N_FREQ, dtype=F32) / N_FREQ)
    coord = jnp.where(jnp.asarray(axis == 0)[None, :],
                      jnp.asarray(pos // GRID_W, F32)[:, None],
                      jnp.asarray(pos % GRID_W, F32)[:, None])
    ang = coord * freqs[jnp.asarray(lane % N_FREQ)][None, :]
    cos = jnp.cos(ang)
    sin = jnp.where(jnp.asarray(second_half)[None, :], jnp.sin(ang), -jnp.sin(ang))
    cos = jnp.concatenate([cos, jnp.ones((n_ctx_rows, LANES), F32)], axis=0)
    sin = jnp.concatenate([sin, jnp.zeros((n_ctx_rows, LANES), F32)], axis=0)
    return cos, sin


def kernel(x, c, ctx, c_ctx, ada_w, ada_b, norm_mix_w, norm_ffn_w, w_in, ret_rate, ret_norm_w,
           swa_sinks, gdn_conv_w, gdn_a_log, gdn_dt_bias, gdn_norm_w, w_out, w_ffn_in, w_ffn_out,
           final_norm_w):
    b, l, d = x.shape
    lc = ctx.shape[1]
    depth = ada_w.shape[0]
    assert l % TILE == 0 and lc == TILE and b % NB == 0 and b + NB <= MOD_ROWS and d % LANES == 0
    n_lat = l // TILE

    x_lat, x_ctx, ctx_tile = x, ctx, 0
    c_pad = jnp.zeros((MOD_ROWS, d), F32).at[:b].set(c).at[b:b + NB].set(c_ctx)
    mod = _ada_call(c_pad, ada_w, ada_b)
    cos_t, sin_t = _rope_tables(l, lc)
    consts = _gdn_consts()
    i = np.arange(RET_W)
    ones_head = jnp.asarray((i[:, None] // HEAD_DIM) == (i[None, :] // HEAD_DIM), BF16)
    log_gamma = jnp.log1p(-jnp.exp2(-ret_rate.astype(F32)))

    for layer in range(depth):
        last = layer == depth - 1
        mod_l = mod[layer].reshape(MOD_ROWS, 1, N_MOD * d)
        w_in_bf = jnp.pad(w_in[layer], ((0, 0), (0, IN_W_PAD - IN_W))).astype(BF16)
        ret, swq, swkv, gqkv, gz, ab = _inproj_call(x_lat, x_ctx, ctx_tile, mod_l, norm_mix_w[layer],
                                                    w_in_bf, cos_t, sin_t, n_lat, b)
        o_rf, o_rb = _ret_call(ret, log_gamma[layer], n_lat)
        n_lat_blk = l // Q_BLOCK
        n_q_blk = n_lat_blk if last else n_lat_blk + lc // Q_BLOCK
        y_s = _swa_call(swq, swkv, swa_sinks[layer], n_lat_blk, n_q_blk)
        qkvn, gb = _gdn_prep_call(gqkv, ab, gdn_conv_w[layer], gdn_a_log[layer], gdn_dt_bias[layer],
                                  ones_head, n_lat)
        o_gf, o_gb = _gdn_call(qkvn, gb, consts, n_lat)
        x_new = _post_call(x_lat, None if last else x_ctx, ctx_tile, o_rf, o_rb, ret, y_s, o_gf, o_gb, gz,
                           mod_l, norm_ffn_w[layer], final_norm_w, ret_norm_w[layer], gdn_norm_w[layer],
                           ones_head, w_out[layer].astype(BF16), w_ffn_in[layer].astype(BF16),
                           w_ffn_out[layer].astype(BF16), n_lat, b, last)
        x_lat, x_ctx, ctx_tile = x_new, x_new, n_lat
    return x_lat
```

```python
import functools

import numpy as np
import jax
import jax.numpy as jnp
from jax import lax
from jax.experimental import pallas as pl
from jax.experimental.pallas import tpu as pltpu

F32 = jnp.float32
BF16 = jnp.bfloat16

HEAD_DIM = 64
N_FREQ = HEAD_DIM // 4
GRID_W = 64
RET_HEADS = 4
SWA_Q_HEADS = 8
SWA_KV_HEADS = 2
GDN_HEADS = 4
RET_W = RET_HEADS * HEAD_DIM
SWA_W = SWA_Q_HEADS * HEAD_DIM
SWA_KV_W = SWA_KV_HEADS * HEAD_DIM
GDN_W = GDN_HEADS * HEAD_DIM
WINDOW = 128
Q_BLOCK = 128
GDN_CHUNK = 64
CONV_W = 3
ROPE_BASE = 10000.0
EPS = 1e-6
NEG_INF = -1e30
N_MOD = 6

LANES = 128
TILE = 256
MOD_ROWS = 16
NB = 2
NB_ROWS = 4
VMEM_LIMIT = 56 * 1024 * 1024

COL_RET = 0
COL_SWA_Q = 4 * RET_W
COL_SWA_KV = COL_SWA_Q + SWA_W
COL_GDN_QKV = COL_SWA_KV + 2 * SWA_KV_W
COL_GDN_Z = COL_GDN_QKV + 3 * GDN_W
COL_AB = COL_GDN_Z + GDN_W
IN_W = COL_AB + 4 * GDN_HEADS
IN_W_PAD = COL_AB + LANES


def _dot(a, b):
    return jnp.dot(a, b, preferred_element_type=F32)


def _dot_nt(a, b):
    return lax.dot_general(a, b, (((1,), (1,)), ((), ())), preferred_element_type=F32)


def _split3(x):
    hi = x.astype(BF16)
    r = x - hi.astype(F32)
    mid = r.astype(BF16)
    lo = (r - mid.astype(F32)).astype(BF16)
    return hi, mid, lo


def _dot_x_sel(x, sel):
    hi, mid, lo = _split3(x)
    return _dot(hi, sel) + _dot(mid, sel) + _dot(lo, sel)


def _dot_sel_x(sel, x):
    hi, mid, lo = _split3(x)
    return _dot(sel, hi) + _dot(sel, mid) + _dot(sel, lo)


def _silu(x):
    return x * jax.nn.sigmoid(x)


def _head_masks(rows, width):
    lane = lax.broadcasted_iota(jnp.int32, (rows, width), 1)
    return [(lane // HEAD_DIM) == h for h in range(width // HEAD_DIM)]


def _head_sum(x, ones_bd):
    return _dot_x_sel(x, ones_bd)


def _cparams(n_axes):
    return pltpu.CompilerParams(dimension_semantics=("arbitrary",) * n_axes,
                                vmem_limit_bytes=VMEM_LIMIT)


def _ada_kernel(c_ref, w_ref, b_ref, o_ref):
    cs = _silu(c_ref[...])
    o_ref[0] = _dot(cs.astype(BF16), w_ref[0].astype(BF16)) + b_ref[0]


def _ada_call(c_pad, ada_w, ada_b):
    depth, d, n = ada_w.shape
    tn = 1536
    return pl.pallas_call(
        _ada_kernel,
        grid=(depth, n // tn),
        in_specs=[pl.BlockSpec((MOD_ROWS, d), lambda l, j: (0, 0)),
                  pl.BlockSpec((1, d, tn), lambda l, j: (l, 0, j)),
                  pl.BlockSpec((1, 1, tn), lambda l, j: (l, 0, j))],
        out_specs=pl.BlockSpec((1, MOD_ROWS, tn), lambda l, j: (l, 0, j)),
        out_shape=jax.ShapeDtypeStruct((depth, MOD_ROWS, n), F32),
        compiler_params=_cparams(2),
        name="ada_mod",
    )(c_pad, ada_w, ada_b.reshape(depth, 1, n))


def _rope(t, cos, sin_signed):
    lane = lax.broadcasted_iota(jnp.int32, t.shape, 1)
    first_half = (lane % (2 * N_FREQ)) < N_FREQ
    partner = jnp.where(first_half, pltpu.roll(t, LANES - N_FREQ, 1), pltpu.roll(t, N_FREQ, 1))
    return t * cos + partner * sin_signed


def _inproj_kernel(n_lat_tiles, x_ref, c_ref, sh_ref, sc_ref, nw_ref, w_ref, cos_ref, sin_ref,
                   ret_ref, swq_ref, swkv_ref, gqkv_ref, gz_ref, ab_ref):
    is_lat = pl.program_id(1) < n_lat_tiles
    hs = []
    for bb in range(NB_ROWS):
        x = jnp.where(is_lat, x_ref[bb], c_ref[bb])
        y = x * lax.rsqrt(jnp.mean(x * x, axis=-1, keepdims=True) + EPS) * nw_ref[...]
        hs.append((y * (1.0 + sc_ref[bb]) + sh_ref[bb]).astype(BF16))
    h = jnp.concatenate(hs, axis=0)
    cos = cos_ref[...]
    sin = sin_ref[...]

    def proj(col, width, out_ref, out_col, roped_lanes):
        r = _dot(h, w_ref[:, col:col + width])
        for bb in range(NB_ROWS):
            for g in range(width // LANES):
                rg = r[bb * TILE:(bb + 1) * TILE, g * LANES:(g + 1) * LANES]
                if g * LANES < roped_lanes:
                    rg = _rope(rg, cos, sin)
                out_ref[bb, :, out_col + g * LANES:out_col + (g + 1) * LANES] = rg

    for g in range(4):
        proj(COL_RET + g * RET_W, RET_W, ret_ref, g * RET_W, RET_W if g < 2 else 0)
    for g in range(SWA_W // TILE):
        proj(COL_SWA_Q + g * TILE, TILE, swq_ref, g * TILE, TILE)
    proj(COL_SWA_KV, 2 * SWA_KV_W, swkv_ref, 0, SWA_KV_W)
    for g in range(3):
        proj(COL_GDN_QKV + g * GDN_W, GDN_W, gqkv_ref, g * GDN_W, 0)
    proj(COL_GDN_Z, GDN_W, gz_ref, 0, 0)
    proj(COL_AB, LANES, ab_ref, 0, 0)


def _row_source_specs(d, n_lat_tiles, ctx_tile, nb):
    return [pl.BlockSpec((nb, TILE, d), lambda bi, t: (bi, jnp.minimum(t, n_lat_tiles - 1), 0)),
            pl.BlockSpec((nb, TILE, d), lambda bi, t: (bi, ctx_tile, 0))]


def _mod_spec(d, k, n_lat_tiles, ctx_row, nb):
    if n_lat_tiles is None:
        return pl.BlockSpec((nb, 1, d), lambda bi, t: (bi, 0, k))
    return pl.BlockSpec((nb, 1, d), lambda bi, t: (jnp.where(t >= n_lat_tiles, ctx_row // nb, bi), 0, k))


def _inproj_call(x_lat, x_ctx, ctx_tile, mod, norm_w, w_in_bf, cos_t, sin_t, n_lat_tiles, ctx_row):
    b, _, d = x_lat.shape
    nt = n_lat_tiles + 1
    tt = nt * TILE
    widths = (4 * RET_W, SWA_W, 2 * SWA_KV_W, 3 * GDN_W, GDN_W, LANES)
    return pl.pallas_call(
        functools.partial(_inproj_kernel, n_lat_tiles),
        grid=(b // NB_ROWS, nt),
        in_specs=_row_source_specs(d, n_lat_tiles, ctx_tile, NB_ROWS) + [
                  _mod_spec(d, 0, n_lat_tiles, ctx_row, NB_ROWS),
                  _mod_spec(d, 1, n_lat_tiles, ctx_row, NB_ROWS),
                  pl.BlockSpec((1, d), lambda bi, t: (0, 0)),
                  pl.BlockSpec((d, IN_W_PAD), lambda bi, t: (0, 0)),
                  pl.BlockSpec((TILE, LANES), lambda bi, t: (t, 0)),
                  pl.BlockSpec((TILE, LANES), lambda bi, t: (t, 0))],
        out_specs=[pl.BlockSpec((NB_ROWS, TILE, w), lambda bi, t: (bi, t, 0)) for w in widths],
        out_shape=[jax.ShapeDtypeStruct((b, tt, w), F32) for w in widths],
        compiler_params=_cparams(2),
        name="in_proj",
    )(x_lat, x_ctx, mod, mod, norm_w.reshape(1, d), w_in_bf, cos_t, sin_t)


def _ret_kernel(lg_ref, lge_ref, rf_ref, rb_ref, of_ref, ob_ref, sf_ref, sb_ref):
    c = pl.program_id(1)

    @pl.when(c == 0)
    def _():
        sf_ref[...] = jnp.zeros_like(sf_ref)
        sb_ref[...] = jnp.zeros_like(sb_ref)

    masks = _head_masks(TILE, RET_W)
    ri = lax.broadcasted_iota(jnp.int32, (TILE, TILE), 0)
    ci = lax.broadcasted_iota(jnp.int32, (TILE, TILE), 1)
    bd = (ri // HEAD_DIM) == (ci // HEAD_DIM)
    idx = ri.astype(F32)
    diff = (ri - ci).astype(F32)
    dm = [jnp.where(diff >= 0, jnp.exp(lg_ref[0, h] * jnp.maximum(diff, 0.0)), 0.0)
          + jnp.where(diff <= 0, jnp.exp(lg_ref[1, h] * jnp.maximum(-diff, 0.0)), 0.0)
          for h in range(RET_HEADS)]
    dec = []
    for direction in (0, 1):
        lge = lge_ref[direction:direction + 1, :]
        if direction == 0:
            dec.append((jnp.exp(lge * (idx + 1.0)), jnp.exp(lge * (TILE - 1.0 - idx)),
                        jnp.exp(lge * float(TILE))))
        else:
            dec.append((jnp.exp(lge * (TILE - idx)), jnp.exp(lge * idx), jnp.exp(lge * float(TILE))))

    def sweep(direction, bb, r_ref, s_ref):
        blk = r_ref[bb]
        q = blk[:, 0:RET_W] * (HEAD_DIM ** -0.5)
        k = blk[:, RET_W:2 * RET_W]
        v = blk[:, 2 * RET_W:3 * RET_W]
        q_dec, k_dec, c_dec = dec[direction]
        s = s_ref[bb]
        cross = _dot((q * q_dec).astype(BF16), s.astype(BF16))
        kt = (k * k_dec).T
        upd = _dot(kt.astype(BF16), v.astype(BF16))
        s_ref[bb] = s * c_dec + jnp.where(bd, upd, 0.0)
        return cross, q, k, v

    for bb in range(NB_ROWS):
        o, q, k, v = sweep(0, bb, rf_ref, sf_ref)
        kb = k.astype(BF16)
        for h in range(RET_HEADS):
            sc = _dot_nt(jnp.where(masks[h], q, 0.0).astype(BF16), kb)
            o = o + _dot((sc * dm[h]).astype(BF16), jnp.where(masks[h], v, 0.0).astype(BF16))
        of_ref[bb] = o
        ob_ref[bb] = sweep(1, bb, rb_ref, sb_ref)[0]


def _ret_call(ret, log_gamma, n_lat):
    b, tt, _ = ret.shape
    nt = tt // TILE
    lge = jnp.repeat(log_gamma, HEAD_DIM, axis=-1)

    def f_idx(bi, c):
        return (bi, jnp.where(c == 0, n_lat, c - 1), 0)

    def b_idx(bi, c):
        return (bi, jnp.where(c == 0, n_lat, n_lat - c), 0)

    return pl.pallas_call(
        _ret_kernel,
        grid=(b // NB_ROWS, nt),
        in_specs=[pl.BlockSpec(memory_space=pltpu.SMEM),
                  pl.BlockSpec((2, RET_W), lambda bi, c: (0, 0)),
                  pl.BlockSpec((NB_ROWS, TILE, 3 * RET_W), f_idx),
                  pl.BlockSpec((NB_ROWS, TILE, 3 * RET_W), b_idx)],
        out_specs=[pl.BlockSpec((NB_ROWS, TILE, RET_W), f_idx), pl.BlockSpec((NB_ROWS, TILE, RET_W), b_idx)],
        out_shape=[jax.ShapeDtypeStruct((b, tt, RET_W), F32)] * 2,
        scratch_shapes=[pltpu.VMEM((NB_ROWS, RET_W, RET_W), F32), pltpu.VMEM((NB_ROWS, RET_W, RET_W), F32)],
        compiler_params=_cparams(2),
        name="retention",
    )(log_gamma, lge, ret, ret)


SWA_BLOCKS_PER_STEP = 2


def _swa_kernel(n_lat_blk, sink_ref, q_ref, kv_ref, y_ref):
    for bb in range(NB):
        for i in range(SWA_BLOCKS_PER_STEP):
            _swa_block(n_lat_blk, sink_ref, q_ref.at[bb:bb + 1], kv_ref.at[bb:bb + 1], y_ref.at[bb:bb + 1],
                       pl.program_id(1) * SWA_BLOCKS_PER_STEP + i, slice(i * Q_BLOCK, (i + 1) * Q_BLOCK))


def _swa_block(n_lat_blk, sink_ref, q_ref, kv_ref, y_ref, j, q_rows_in_step):
    is_lat = j < n_lat_blk
    jc = jnp.minimum(j, n_lat_blk - 1)
    jp = jnp.maximum(jc - 1, 0)
    jn = jnp.minimum(jc + 1, n_lat_blk - 1)
    lat_rows = n_lat_blk * Q_BLOCK

    def piece(blk):
        return kv_ref[0, pl.ds(pl.multiple_of(blk * Q_BLOCK, Q_BLOCK), Q_BLOCK), :]

    kv_all = jnp.concatenate([piece(jp), piece(jc), piece(jn), kv_ref[0, lat_rows:, :]], axis=0)
    k_all = kv_all[:, 0:SWA_KV_W].astype(BF16)
    v_all = kv_all[:, SWA_KV_W:2 * SWA_KV_W].astype(BF16)
    n_piece = k_all.shape[0] // Q_BLOCK

    group = SWA_Q_HEADS // SWA_KV_HEADS
    rows = group * Q_BLOCK
    ql = lax.broadcasted_iota(jnp.int32, (rows, Q_BLOCK), 0) % Q_BLOCK
    kl = lax.broadcasted_iota(jnp.int32, (rows, Q_BLOCK), 1)
    ok_prev = (kl >= ql) & is_lat & (j >= 1)
    ok_next = (kl <= ql) & is_lat & (j <= n_lat_blk - 2)
    lane = lax.broadcasted_iota(jnp.int32, (Q_BLOCK, LANES), 1)
    low = lane < HEAD_DIM
    out_tiles = [[None, None] for _ in range(SWA_W // LANES)]

    for g in range(SWA_KV_HEADS):
        q_rows, sink_rows = [], []
        for hh in range(group):
            head = g * group + hh
            t, e = head // 2, head % 2
            qt = q_ref[0, q_rows_in_step, t * LANES:(t + 1) * LANES] * (HEAD_DIM ** -0.5)
            qm = jnp.where(low if e == 0 else ~low, qt, 0.0)
            if e != g:
                qm = pltpu.roll(qm, HEAD_DIM, 1)
            q_rows.append(qm)
            sink_rows.append(jnp.full((Q_BLOCK, 1), sink_ref[0, head], F32))
        qs = jnp.concatenate(q_rows, axis=0).astype(BF16)
        sink = jnp.concatenate(sink_rows, axis=0)
        s = _dot_nt(qs, k_all)

        def masked(i):
            s_i = s[:, i * Q_BLOCK:(i + 1) * Q_BLOCK]
            if i == 0:
                return jnp.where(ok_prev, s_i, NEG_INF)
            if i == 1:
                return jnp.where(is_lat, s_i, NEG_INF)
            if i == 2:
                return jnp.where(ok_next, s_i, NEG_INF)
            return s_i

        mx = masked(0)
        for i in range(1, n_piece):
            mx = jnp.maximum(mx, masked(i))
        m = jnp.maximum(jnp.max(mx, axis=-1, keepdims=True), sink)
        pr = [jnp.exp(masked(i) - m) for i in range(n_piece)]
        acc = pr[0]
        for p_i in pr[1:]:
            acc = acc + p_i
        den = jnp.sum(acc, axis=-1, keepdims=True) + jnp.exp(sink - m)
        pb = jnp.concatenate([p_i.astype(BF16) for p_i in pr], axis=1)
        o = _dot(pb, v_all) / den
        for hh in range(group):
            head = g * group + hh
            t, e = head // 2, head % 2
            oh = o[hh * Q_BLOCK:(hh + 1) * Q_BLOCK]
            if e != g:
                oh = pltpu.roll(oh, HEAD_DIM, 1)
            out_tiles[t][e] = oh
    for t in range(SWA_W // LANES):
        y_ref[0, q_rows_in_step, t * LANES:(t + 1) * LANES] = jnp.where(low, out_tiles[t][0], out_tiles[t][1])


def _swa_call(swq, swkv, sinks, n_lat_blk, n_q_blk):
    b, tt, _ = swq.shape
    step_rows = SWA_BLOCKS_PER_STEP * Q_BLOCK
    assert n_q_blk % SWA_BLOCKS_PER_STEP == 0
    return pl.pallas_call(
        functools.partial(_swa_kernel, n_lat_blk),
        grid=(b // NB, n_q_blk // SWA_BLOCKS_PER_STEP),
        in_specs=[pl.BlockSpec(memory_space=pltpu.SMEM),
                  pl.BlockSpec((NB, step_rows, SWA_W), lambda bi, j: (bi, j, 0)),
                  pl.BlockSpec((NB, tt, 2 * SWA_KV_W), lambda bi, j: (bi, 0, 0))],
        out_specs=pl.BlockSpec((NB, step_rows, SWA_W), lambda bi, j: (bi, j, 0)),
        out_shape=jax.ShapeDtypeStruct((b, n_q_blk * Q_BLOCK, SWA_W), F32),
        compiler_params=_cparams(2),
        name="swa",
    )(sinks.reshape(1, SWA_Q_HEADS), swq, swkv)


def _gdn_prep_kernel(n_lat, x_ref, hp_ref, hn_ref, ab_ref, cw_ref, alog_ref, dtb_ref, ones_ref,
                     qkv_ref, gb_ref):
    t = pl.program_id(1)
    has_prev = ((t > 0) & (t < n_lat)).astype(F32)
    has_next = (t < n_lat - 1).astype(F32)
    ones_bd = ones_ref[...]

    def l2n(u):
        return u * lax.rsqrt(_head_sum(u * u, ones_bd) + EPS)

    for bb in range(NB_ROWS):
        x = x_ref[bb]
        row = lax.broadcasted_iota(jnp.int32, x.shape, 0)
        prev_row = hp_ref[bb, 7:8, :] * has_prev
        next_row = hn_ref[bb, 0:1, :] * has_next
        xm1 = jnp.where(row == 0, prev_row, pltpu.roll(x, 1, 0))
        xp1 = jnp.where(row == TILE - 1, next_row, pltpu.roll(x, TILE - 1, 0))
        conv = xm1 * cw_ref[0:1, :] + x * cw_ref[1:2, :] + xp1 * cw_ref[2:3, :]
        s = _silu(conv)
        qkv_ref[bb, :, 0:GDN_W] = l2n(s[:, 0:GDN_W]) * (HEAD_DIM ** -0.5)
        qkv_ref[bb, :, GDN_W:2 * GDN_W] = l2n(s[:, GDN_W:2 * GDN_W])
        qkv_ref[bb, :, 2 * GDN_W:3 * GDN_W] = s[:, 2 * GDN_W:3 * GDN_W]

        ab = ab_ref[bb]
        lane = lax.broadcasted_iota(jnp.int32, ab.shape, 1)
        g = -jnp.exp(alog_ref[...]) * jax.nn.softplus(ab + dtb_ref[...])
        beta = jax.nn.sigmoid(ab)
        gb_ref[bb] = jnp.where(lane < 2 * GDN_HEADS, g, jnp.where(lane < 4 * GDN_HEADS, beta, 0.0))


def _gdn_prep_call(gqkv, ab, conv_w, a_log, dt_bias, ones_bd, n_lat):
    b, tt, w = gqkv.shape
    nt = tt // TILE
    sub = 8
    per = TILE // sub
    last = tt // sub - 1
    pad = LANES - 2 * GDN_HEADS
    alog_pad = jnp.pad(a_log.reshape(1, 2 * GDN_HEADS), ((0, 0), (0, pad)))
    dtb_pad = jnp.pad(dt_bias.reshape(1, 2 * GDN_HEADS), ((0, 0), (0, pad)))
    return pl.pallas_call(
        functools.partial(_gdn_prep_kernel, n_lat),
        grid=(b // NB_ROWS, nt),
        in_specs=[pl.BlockSpec((NB_ROWS, TILE, w), lambda bi, t: (bi, t, 0)),
                  pl.BlockSpec((NB_ROWS, sub, w), lambda bi, t: (bi, jnp.maximum(t * per - 1, 0), 0)),
                  pl.BlockSpec((NB_ROWS, sub, w), lambda bi, t: (bi, jnp.minimum((t + 1) * per, last), 0)),
                  pl.BlockSpec((NB_ROWS, TILE, LANES), lambda bi, t: (bi, t, 0)),
                  pl.BlockSpec((CONV_W, w), lambda bi, t: (0, 0)),
                  pl.BlockSpec((1, LANES), lambda bi, t: (0, 0)),
                  pl.BlockSpec((1, LANES), lambda bi, t: (0, 0)),
                  pl.BlockSpec((GDN_W, GDN_W), lambda bi, t: (0, 0))],
        out_specs=[pl.BlockSpec((NB_ROWS, TILE, w), lambda bi, t: (bi, t, 0)),
                   pl.BlockSpec((NB_ROWS, TILE, LANES), lambda bi, t: (bi, t, 0))],
        out_shape=[jax.ShapeDtypeStruct((b, tt, w), F32), jax.ShapeDtypeStruct((b, tt, LANES), F32)],
        compiler_params=_cparams(2),
        name="gdn_prep",
    )(gqkv, gqkv, gqkv, ab, conv_w, alog_pad, dtb_pad, ones_bd)


N_SUB = TILE // GDN_CHUNK


def _gdn_kernel(n_lat, ones_ref, tri_ref, xf_ref, gbf_ref, xb_ref, gbb_ref,
                of_ref, ob_ref, sf_ref, sb_ref):
    c = pl.program_id(1)

    @pl.when(c == 0)
    def _():
        sf_ref[...] = jnp.zeros_like(sf_ref)
        sb_ref[...] = jnp.zeros_like(sb_ref)

    ones_bd = ones_ref[...]
    masks = _head_masks(TILE, GDN_W)
    sub_masks = _head_masks(GDN_CHUNK, GDN_W)
    ri = lax.broadcasted_iota(jnp.int32, (TILE, TILE), 0)
    ci = lax.broadcasted_iota(jnp.int32, (TILE, TILE), 1)
    bd = (ri // GDN_CHUNK) == (ci // GDN_CHUNK)
    x_refs = (xf_ref, xb_ref)
    gb_refs = (gbf_ref, gbb_ref)
    s_refs = (sf_ref, sb_ref)
    o_refs = (of_ref, ob_ref)
    dirs = tuple(range(2 * NB))

    def expand(x, first_col):
        e = jnp.zeros((TILE, GDN_W), F32)
        for h in range(GDN_HEADS):
            e = jnp.where(masks[h], x[:, first_col + h:first_col + h + 1], e)
        return e

    pre = []
    for d in dirs:
        blk = x_refs[d % 2][d // 2]
        qn = blk[:, 0:GDN_W]
        kn = blk[:, GDN_W:2 * GDN_W]
        v = blk[:, 2 * GDN_W:3 * GDN_W]
        gbv = gb_refs[d % 2][d // 2]
        gc = _dot_sel_x(tri_ref[d % 2], gbv)
        gtot = _dot_sel_x(ones_bd, gbv)
        gc_e = expand(gc, (d % 2) * GDN_HEADS)
        gtot_e = expand(gtot, (d % 2) * GDN_HEADS)
        beta_e = expand(gbv, 2 * GDN_HEADS + (d % 2) * GDN_HEADS)
        egc = jnp.exp(gc_e)
        pre.append(dict(qn=qn, kn=kn, kn_b=kn.astype(BF16), gb_t=gbv.T, gc=gc, gc_t=gc.T,
                        vb_t=(v * beta_e).T, kb_t=(kn * beta_e * egc).T, qd=qn * egc,
                        kt=kn * jnp.exp(gtot_e - gc_e), gtot_dec=jnp.exp(gtot_e),
                        causal=bd & ((ri >= ci) if d % 2 == 0 else (ri <= ci)),
                        causal_t=bd & ((ri <= ci) if d % 2 == 0 else (ri >= ci))))

    chains = [(d, h) for h in range(GDN_HEADS) for d in dirs]
    m_t, z_acc, attn = {}, {}, {}
    for (d, h) in chains:
        p = pre[d]
        r = (d % 2) * GDN_HEADS + h
        kk = _dot_nt(jnp.where(masks[h], p["kn"], 0.0).astype(BF16), p["kn_b"])
        qk = _dot_nt(jnp.where(masks[h], p["qn"], 0.0).astype(BF16), p["kn_b"])
        diff = p["gc"][:, r:r + 1] - p["gc_t"][r:r + 1, :]
        attn[(d, h)] = qk * jnp.exp(jnp.where(p["causal"], diff, NEG_INF))
        brow = p["gb_t"][2 * GDN_HEADS + r:2 * GDN_HEADS + r + 1, :]
        m_t[(d, h)] = jnp.where(ri == ci, 0.0,
                                kk * brow * jnp.exp(jnp.where(p["causal_t"], -diff, NEG_INF)))
        z_acc[(d, h)] = jnp.concatenate([p["vb_t"][h * HEAD_DIM:(h + 1) * HEAD_DIM],
                                         p["kb_t"][h * HEAD_DIM:(h + 1) * HEAD_DIM]], axis=0)

    def off_diag(s_blk):
        return ((ri // (2 * s_blk)) == (ci // (2 * s_blk))) & ((ri // s_blk) != (ci // s_blk))

    x_inv = {}
    for ch in chains:
        x_inv[ch] = (ri == ci).astype(F32) - jnp.where(off_diag(1), m_t[ch], 0.0)
    s_blk = 2
    while s_blk < GDN_CHUNK // 2:
        for ch in chains:
            xb = x_inv[ch].astype(BF16)
            pc = _dot(xb, jnp.where(off_diag(s_blk), m_t[ch], 0.0).astype(BF16))
            x_inv[ch] = x_inv[ch] - _dot(pc.astype(BF16), xb)
        s_blk *= 2
    for ch in chains:
        xb = x_inv[ch].astype(BF16)
        y = _dot(z_acc[ch].astype(BF16), xb)
        yc = _dot(y.astype(BF16), jnp.where(off_diag(s_blk), m_t[ch], 0.0).astype(BF16))
        z_acc[ch] = y - _dot(yc.astype(BF16), xb)

    u = [jnp.concatenate([z_acc[(d, h)][0:HEAD_DIM] for h in range(GDN_HEADS)], axis=0).T for d in dirs]
    w = [jnp.concatenate([z_acc[(d, h)][HEAD_DIM:2 * HEAD_DIM] for h in range(GDN_HEADS)], axis=0).T
         for d in dirs]

    state_bd = (ri // HEAD_DIM) == (ci // HEAD_DIM)
    kt_t = [pre[d]["kt"].T for d in dirs]
    s = [s_refs[d % 2][d // 2] for d in dirs]
    outs = [[None] * N_SUB for _ in dirs]
    for i in range(N_SUB):
        for d in dirs:
            sub = i if d % 2 == 0 else N_SUB - 1 - i
            r0 = sub * GDN_CHUNK
            wq = jnp.concatenate([w[d][r0:r0 + GDN_CHUNK], pre[d]["qd"][r0:r0 + GDN_CHUNK]], axis=0)
            ws = _dot(wq.astype(BF16), s[d].astype(BF16))
            v_new = u[d][r0:r0 + GDN_CHUNK] - ws[0:GDN_CHUNK]
            o_sub = ws[GDN_CHUNK:2 * GDN_CHUNK]
            for h in range(GDN_HEADS):
                a_blk = attn[(d, h)][r0:r0 + GDN_CHUNK, r0:r0 + GDN_CHUNK].astype(BF16)
                o_sub = o_sub + _dot(a_blk, jnp.where(sub_masks[h], v_new, 0.0).astype(BF16))
            outs[d][sub] = o_sub
            upd = _dot(kt_t[d][:, r0:r0 + GDN_CHUNK].astype(BF16), v_new.astype(BF16))
            s[d] = s[d] * pre[d]["gtot_dec"][r0:r0 + 1, :] + jnp.where(state_bd, upd, 0.0)
    for d in dirs:
        s_refs[d % 2][d // 2] = s[d]
        o_refs[d % 2][d // 2] = jnp.concatenate(outs[d], axis=0)


def _gdn_consts():
    i = np.arange(TILE)
    same = (i[:, None] // GDN_CHUNK) == (i[None, :] // GDN_CHUNK)
    tri = np.stack([same & (i[None, :] <= i[:, None]), same & (i[None, :] >= i[:, None])])
    return jnp.asarray(same, BF16), jnp.asarray(tri, BF16)


def _gdn_call(qkvn, gb, consts, n_lat):
    b, tt, w = qkvn.shape
    nt = tt // TILE
    ones_chunk, tri = consts

    def f_idx(bi, c):
        return (bi, jnp.where(c == 0, n_lat, c - 1), 0)

    def b_idx(bi, c):
        return (bi, jnp.where(c == 0, n_lat, n_lat - c), 0)

    return pl.pallas_call(
        functools.partial(_gdn_kernel, n_lat),
        grid=(b // NB, nt),
        in_specs=[pl.BlockSpec((TILE, TILE), lambda bi, c: (0, 0)),
                  pl.BlockSpec((2, TILE, TILE), lambda bi, c: (0, 0, 0)),
                  pl.BlockSpec((NB, TILE, w), f_idx),
                  pl.BlockSpec((NB, TILE, LANES), f_idx),
                  pl.BlockSpec((NB, TILE, w), b_idx),
                  pl.BlockSpec((NB, TILE, LANES), b_idx)],
        out_specs=[pl.BlockSpec((NB, TILE, GDN_W), f_idx), pl.BlockSpec((NB, TILE, GDN_W), b_idx)],
        out_shape=[jax.ShapeDtypeStruct((b, tt, GDN_W), F32)] * 2,
        scratch_shapes=[pltpu.VMEM((NB, GDN_W, GDN_W), F32), pltpu.VMEM((NB, GDN_W, GDN_W), F32)],
        compiler_params=_cparams(2),
        name="gdn",
    )(ones_chunk, tri, qkvn, gb, qkvn, gb)


FF_CHUNK = 512


def _post_kernel(d_ff, final, n_lat_tiles, with_ctx, x_ref, *refs):
    if with_ctx:
        c_ref, refs = refs[0], refs[1:]
        is_lat = pl.program_id(1) < n_lat_tiles
        xs = [jnp.where(is_lat, x_ref[bb], c_ref[bb]) for bb in range(NB)]
    else:
        xs = [x_ref[bb] for bb in range(NB)]
    (rf_ref, rb_ref, rg_ref, ys_ref, gf_ref, gb_ref, gz_ref, g1_ref, sh_ref, sc_ref, g2_ref, nw_ref,
     fw_ref, rnw_ref, gnw_ref, ones_ref, wo_ref, w1_ref, w2_ref, o_ref) = refs
    ones_head = ones_ref[...]

    def rows(per_sample):
        return jnp.concatenate([per_sample(bb) for bb in range(NB)], axis=0)

    def head_out(f_ref, b_ref, gate_ref, w_ref):
        def one(bb):
            o = f_ref[bb] + b_ref[bb]
            ms = _head_sum(o * o, ones_head) * (1.0 / HEAD_DIM)
            return (o * lax.rsqrt(ms + EPS) * w_ref[...] * _silu(gate_ref[bb])).astype(BF16)
        return rows(one)

    mix = (_dot(head_out(rf_ref, rb_ref, rg_ref, rnw_ref), wo_ref[0:RET_W, :])
           + _dot(rows(lambda bb: ys_ref[bb].astype(BF16)), wo_ref[RET_W:RET_W + SWA_W, :])
           + _dot(head_out(gf_ref, gb_ref, gz_ref, gnw_ref), wo_ref[RET_W + SWA_W:, :]))
    x1 = [xs[bb] + g1_ref[bb] * mix[bb * TILE:(bb + 1) * TILE] for bb in range(NB)]

    def ffn_in(bb):
        y = x1[bb] * lax.rsqrt(jnp.mean(x1[bb] * x1[bb], axis=-1, keepdims=True) + EPS) * nw_ref[...]
        return (y * (1.0 + sc_ref[bb]) + sh_ref[bb]).astype(BF16)

    h = rows(ffn_in)
    acc = jnp.zeros((NB * TILE, x1[0].shape[1]), F32)
    for c0 in range(0, d_ff, FF_CHUNK):
        cw = min(FF_CHUNK, d_ff - c0)
        gate = _dot(h, w1_ref[:, c0:c0 + cw])
        up = _dot(h, w1_ref[:, d_ff + c0:d_ff + c0 + cw])
        acc = acc + _dot((_silu(gate) * up).astype(BF16), w2_ref[c0:c0 + cw, :])
    for bb in range(NB):
        x2 = x1[bb] + g2_ref[bb] * acc[bb * TILE:(bb + 1) * TILE]
        if final:
            x2 = x2 * lax.rsqrt(jnp.mean(x2 * x2, axis=-1, keepdims=True) + EPS) * fw_ref[...]
        o_ref[bb] = x2


def _post_call(x_lat, x_ctx, ctx_tile, o_rf, o_rb, ret, y_s, o_gf, o_gb, gz, mod, norm_w, final_w,
               ret_norm_w, gdn_norm_w, ones_head, wo_bf, w1_bf, w2_bf, n_lat_tiles, ctx_row, final):
    b, _, d = x_lat.shape
    d_ff = w2_bf.shape[0]
    with_ctx = x_ctx is not None
    n_steps = n_lat_tiles + 1 if with_ctx else n_lat_tiles

    def mod_spec(k):
        return _mod_spec(d, k, n_lat_tiles if with_ctx else None, ctx_row, NB)

    def tile_spec(w):
        return pl.BlockSpec((NB, TILE, w), lambda bi, t: (bi, t, 0))

    full = lambda shape: pl.BlockSpec(shape, lambda bi, t: (0,) * len(shape))
    x_specs = _row_source_specs(d, n_lat_tiles, ctx_tile, NB) if with_ctx else [tile_spec(d)]
    x_args = (x_lat, x_ctx) if with_ctx else (x_lat,)
    return pl.pallas_call(
        functools.partial(_post_kernel, d_ff, final, n_lat_tiles, with_ctx),
        grid=(b // NB, n_steps),
        in_specs=x_specs + [
                  tile_spec(RET_W), tile_spec(RET_W),
                  pl.BlockSpec((NB, TILE, RET_W), lambda bi, t: (bi, t, 3)),
                  tile_spec(SWA_W), tile_spec(GDN_W), tile_spec(GDN_W), tile_spec(GDN_W),
                  mod_spec(2), mod_spec(3), mod_spec(4), mod_spec(5),
                  full((1, d)), full((1, d)), full((1, RET_W)), full((1, GDN_W)), full(ones_head.shape),
                  full(wo_bf.shape), full(w1_bf.shape), full(w2_bf.shape)],
        out_specs=tile_spec(d),
        out_shape=jax.ShapeDtypeStruct((b, n_steps * TILE, d), F32),
        compiler_params=_cparams(2),
        name="out_proj_ffn",
    )(*x_args, o_rf, o_rb, ret, y_s, o_gf, o_gb, gz, mod, mod, mod, mod, norm_w.reshape(1, d),
      final_w.reshape(1, d), ret_norm_w.reshape(1, RET_W),
      jnp.tile(gdn_norm_w, GDN_HEADS).reshape(1, GDN_W), ones_head, wo_bf, w1_bf, w2_bf)


def _rope_tables(n_lat_rows, n_ctx_rows):
    pos = np.arange(n_lat_rows)
    lane = np.arange(LANES) % HEAD_DIM
    axis = lane // (2 * N_FREQ)
    second_half = (lane % (2 * N_FREQ)) >= N_FREQ
    freqs = jnp.asarray(ROPE_BASE, F32) ** (-jnp.arange(N_FREQ, dtype=F32) / N_FREQ)
    coord = jnp.where(jnp.asarray(axis == 0)[None, :],
                      jnp.asarray(pos // GRID_W, F32)[:, None],
                      jnp.asarray(pos % GRID_W, F32)[:, None])
    ang = coord * freqs[jnp.asarray(lane % N_FREQ)][None, :]
    cos = jnp.cos(ang)
    sin = jnp.where(jnp.asarray(second_half)[None, :], jnp.sin(ang), -jnp.sin(ang))
    cos = jnp.concatenate([cos, jnp.ones((n_ctx_rows, LANES), F32)], axis=0)
    sin = jnp.concatenate([sin, jnp.zeros((n_ctx_rows, LANES), F32)], axis=0)
    return cos, sin


def kernel(x, c, ctx, c_ctx, ada_w, ada_b, norm_mix_w, norm_ffn_w, w_in, ret_rate, ret_norm_w,
           swa_sinks, gdn_conv_w, gdn_a_log, gdn_dt_bias, gdn_norm_w, w_out, w_ffn_in, w_ffn_out,
           final_norm_w):
    b, l, d = x.shape
    lc = ctx.shape[1]
    depth = ada_w.shape[0]
    assert l % TILE == 0 and lc == TILE and d % LANES == 0
    assert b % NB == 0 and b % NB_ROWS == 0 and NB_ROWS % NB == 0 and b + NB_ROWS <= MOD_ROWS
    n_lat = l // TILE

    x_lat, x_ctx, ctx_tile = x, ctx, 0
    c_pad = jnp.zeros((MOD_ROWS, d), F32).at[:b].set(c).at[b:b + NB_ROWS].set(c_ctx)
    mod = _ada_call(c_pad, ada_w, ada_b)
    cos_t, sin_t = _rope_tables(l, lc)
    consts = _gdn_consts()
    i = np.arange(RET_W)
    ones_head = jnp.asarray((i[:, None] // HEAD_DIM) == (i[None, :] // HEAD_DIM), BF16)
    log_gamma = jnp.log1p(-jnp.exp2(-ret_rate.astype(F32)))

    for layer in range(depth):
        last = layer == depth - 1
        mod_l = mod[layer].reshape(MOD_ROWS, 1, N_MOD * d)
        w_in_bf = jnp.pad(w_in[layer], ((0, 0), (0, IN_W_PAD - IN_W))).astype(BF16)
        ret, swq, swkv, gqkv, gz, ab = _inproj_call(x_lat, x_ctx, ctx_tile, mod_l, norm_mix_w[layer],
                                                    w_in_bf, cos_t, sin_t, n_lat, b)
        o_rf, o_rb = _ret_call(ret, log_gamma[layer], n_lat)
        n_lat_blk = l // Q_BLOCK
        n_q_blk = n_lat_blk if last else n_lat_blk + lc // Q_BLOCK
        y_s = _swa_call(swq, swkv, swa_sinks[layer], n_lat_blk, n_q_blk)
        qkvn, gb = _gdn_prep_call(gqkv, ab, gdn_conv_w[layer], gdn_a_log[layer], gdn_dt_bias[layer],
                                  ones_head, n_lat)
        o_gf, o_gb = _gdn_call(qkvn, gb, consts, n_lat)
        x_new = _post_call(x_lat, None if last else x_ctx, ctx_tile, o_rf, o_rb, ret, y_s, o_gf, o_gb, gz,
                           mod_l, norm_ffn_w[layer], final_norm_w, ret_norm_w[layer], gdn_norm_w[layer],
                           ones_head, w_out[layer].astype(BF16), w_ffn_in[layer].astype(BF16),
                           w_ffn_out[layer].astype(BF16), n_lat, b, last)
        x_lat, x_ctx, ctx_tile = x_new, x_new, n_lat
    return x_lat
```

```python
import functools

import numpy as np
import jax
import jax.numpy as jnp
from jax import lax
from jax.experimental import pallas as pl
from jax.experimental.pallas import tpu as pltpu

F32 = jnp.float32
BF16 = jnp.bfloat16

HEAD_DIM = 64
N_FREQ = HEAD_DIM // 4
GRID_W = 64
RET_HEADS = 4
SWA_Q_HEADS = 8
SWA_KV_HEADS = 2
GDN_HEADS = 4
RET_W = RET_HEADS * HEAD_DIM
SWA_W = SWA_Q_HEADS * HEAD_DIM
SWA_KV_W = SWA_KV_HEADS * HEAD_DIM
GDN_W = GDN_HEADS * HEAD_DIM
WINDOW = 128
Q_BLOCK = 128
GDN_CHUNK = 64
CONV_W = 3
ROPE_BASE = 10000.0
EPS = 1e-6
NEG_INF = -1e30
N_MOD = 6

LANES = 128
TILE = 256
MOD_ROWS = 16
NB = 2
NB_ROWS = 4
NB_WIDE = 8
VMEM_LIMIT = 56 * 1024 * 1024

COL_RET = 0
COL_SWA_Q = 4 * RET_W
COL_SWA_KV = COL_SWA_Q + SWA_W
COL_GDN_QKV = COL_SWA_KV + 2 * SWA_KV_W
COL_GDN_Z = COL_GDN_QKV + 3 * GDN_W
COL_AB = COL_GDN_Z + GDN_W
IN_W = COL_AB + 4 * GDN_HEADS
IN_W_PAD = COL_AB + LANES


def _dot(a, b):
    return jnp.dot(a, b, preferred_element_type=F32)


def _dot_nt(a, b):
    return lax.dot_general(a, b, (((1,), (1,)), ((), ())), preferred_element_type=F32)


def _split3(x):
    hi = x.astype(BF16)
    r = x - hi.astype(F32)
    mid = r.astype(BF16)
    lo = (r - mid.astype(F32)).astype(BF16)
    return hi, mid, lo


def _dot_x_sel(x, sel):
    hi, mid, lo = _split3(x)
    return _dot(hi, sel) + _dot(mid, sel) + _dot(lo, sel)


def _dot_sel_x(sel, x):
    hi, mid, lo = _split3(x)
    return _dot(sel, hi) + _dot(sel, mid) + _dot(sel, lo)


def _silu(x):
    return x * jax.nn.sigmoid(x)


def _head_masks(rows, width):
    lane = lax.broadcasted_iota(jnp.int32, (rows, width), 1)
    return [(lane // HEAD_DIM) == h for h in range(width // HEAD_DIM)]


def _head_sum(x, ones_bd):
    return _dot_x_sel(x, ones_bd)


def _cparams(n_axes):
    return pltpu.CompilerParams(dimension_semantics=("arbitrary",) * n_axes,
                                vmem_limit_bytes=VMEM_LIMIT)


def _ada_kernel(c_ref, w_ref, b_ref, o_ref):
    cs = _silu(c_ref[...])
    o_ref[0] = _dot(cs.astype(BF16), w_ref[0].astype(BF16)) + b_ref[0]


def _ada_call(c_pad, ada_w, ada_b):
    depth, d, n = ada_w.shape
    tn = 1536
    return pl.pallas_call(
        _ada_kernel,
        grid=(depth, n // tn),
        in_specs=[pl.BlockSpec((MOD_ROWS, d), lambda l, j: (0, 0)),
                  pl.BlockSpec((1, d, tn), lambda l, j: (l, 0, j)),
                  pl.BlockSpec((1, 1, tn), lambda l, j: (l, 0, j))],
        out_specs=pl.BlockSpec((1, MOD_ROWS, tn), lambda l, j: (l, 0, j)),
        out_shape=jax.ShapeDtypeStruct((depth, MOD_ROWS, n), F32),
        compiler_params=_cparams(2),
        name="ada_mod",
    )(c_pad, ada_w, ada_b.reshape(depth, 1, n))


def _rope(t, cos, sin_signed):
    lane = lax.broadcasted_iota(jnp.int32, t.shape, 1)
    first_half = (lane % (2 * N_FREQ)) < N_FREQ
    partner = jnp.where(first_half, pltpu.roll(t, LANES - N_FREQ, 1), pltpu.roll(t, N_FREQ, 1))
    return t * cos + partner * sin_signed


def _inproj_kernel(n_lat_tiles, x_ref, c_ref, sh_ref, sc_ref, nw_ref, w_ref, cos_ref, sin_ref,
                   ret_ref, swq_ref, swkv_ref, gqkv_ref, gz_ref, ab_ref):
    is_lat = pl.program_id(1) < n_lat_tiles
    hs = []
    for bb in range(NB_ROWS):
        x = jnp.where(is_lat, x_ref[bb], c_ref[bb])
        y = x * lax.rsqrt(jnp.mean(x * x, axis=-1, keepdims=True) + EPS) * nw_ref[...]
        hs.append((y * (1.0 + sc_ref[bb]) + sh_ref[bb]).astype(BF16))
    h = jnp.concatenate(hs, axis=0)
    cos = cos_ref[...]
    sin = sin_ref[...]

    def proj(col, width, out_ref, out_col, roped_lanes):
        r = _dot(h, w_ref[:, col:col + width])
        for bb in range(NB_ROWS):
            for g in range(width // LANES):
                rg = r[bb * TILE:(bb + 1) * TILE, g * LANES:(g + 1) * LANES]
                if g * LANES < roped_lanes:
                    rg = _rope(rg, cos, sin)
                out_ref[bb, :, out_col + g * LANES:out_col + (g + 1) * LANES] = rg

    for g in range(4):
        proj(COL_RET + g * RET_W, RET_W, ret_ref, g * RET_W, RET_W if g < 2 else 0)
    for g in range(SWA_W // TILE):
        proj(COL_SWA_Q + g * TILE, TILE, swq_ref, g * TILE, TILE)
    proj(COL_SWA_KV, 2 * SWA_KV_W, swkv_ref, 0, SWA_KV_W)
    for g in range(3):
        proj(COL_GDN_QKV + g * GDN_W, GDN_W, gqkv_ref, g * GDN_W, 0)
    proj(COL_GDN_Z, GDN_W, gz_ref, 0, 0)
    proj(COL_AB, LANES, ab_ref, 0, 0)


def _row_source_specs(d, n_lat_tiles, ctx_tile, nb):
    return [pl.BlockSpec((nb, TILE, d), lambda bi, t: (bi, jnp.minimum(t, n_lat_tiles - 1), 0)),
            pl.BlockSpec((nb, TILE, d), lambda bi, t: (bi, ctx_tile, 0))]


def _mod_spec(d, k, n_lat_tiles, ctx_row, nb):
    if n_lat_tiles is None:
        return pl.BlockSpec((nb, 1, d), lambda bi, t: (bi, 0, k))
    return pl.BlockSpec((nb, 1, d), lambda bi, t: (jnp.where(t >= n_lat_tiles, ctx_row // nb, bi), 0, k))


def _inproj_call(x_lat, x_ctx, ctx_tile, mod, norm_w, w_in_bf, cos_t, sin_t, n_lat_tiles, ctx_row):
    b, _, d = x_lat.shape
    nt = n_lat_tiles + 1
    tt = nt * TILE
    widths = (4 * RET_W, SWA_W, 2 * SWA_KV_W, 3 * GDN_W, GDN_W, LANES)
    return pl.pallas_call(
        functools.partial(_inproj_kernel, n_lat_tiles),
        grid=(b // NB_ROWS, nt),
        in_specs=_row_source_specs(d, n_lat_tiles, ctx_tile, NB_ROWS) + [
                  _mod_spec(d, 0, n_lat_tiles, ctx_row, NB_ROWS),
                  _mod_spec(d, 1, n_lat_tiles, ctx_row, NB_ROWS),
                  pl.BlockSpec((1, d), lambda bi, t: (0, 0)),
                  pl.BlockSpec((d, IN_W_PAD), lambda bi, t: (0, 0)),
                  pl.BlockSpec((TILE, LANES), lambda bi, t: (t, 0)),
                  pl.BlockSpec((TILE, LANES), lambda bi, t: (t, 0))],
        out_specs=[pl.BlockSpec((NB_ROWS, TILE, w), lambda bi, t: (bi, t, 0)) for w in widths],
        out_shape=[jax.ShapeDtypeStruct((b, tt, w), F32) for w in widths],
        compiler_params=_cparams(2),
        name="in_proj",
    )(x_lat, x_ctx, mod, mod, norm_w.reshape(1, d), w_in_bf, cos_t, sin_t)


def _ret_kernel(lg_ref, lge_ref, rf_ref, rb_ref, of_ref, ob_ref, sf_ref, sb_ref):
    c = pl.program_id(1)

    @pl.when(c == 0)
    def _():
        sf_ref[...] = jnp.zeros_like(sf_ref)
        sb_ref[...] = jnp.zeros_like(sb_ref)

    masks = _head_masks(TILE, RET_W)
    ri = lax.broadcasted_iota(jnp.int32, (TILE, TILE), 0)
    ci = lax.broadcasted_iota(jnp.int32, (TILE, TILE), 1)
    bd = (ri // HEAD_DIM) == (ci // HEAD_DIM)
    idx = ri.astype(F32)
    diff = (ri - ci).astype(F32)
    dm = [jnp.where(diff >= 0, jnp.exp(lg_ref[0, h] * jnp.maximum(diff, 0.0)), 0.0)
          + jnp.where(diff <= 0, jnp.exp(lg_ref[1, h] * jnp.maximum(-diff, 0.0)), 0.0)
          for h in range(RET_HEADS)]
    dec = []
    for direction in (0, 1):
        lge = lge_ref[direction:direction + 1, :]
        if direction == 0:
            dec.append((jnp.exp(lge * (idx + 1.0)), jnp.exp(lge * (TILE - 1.0 - idx)),
                        jnp.exp(lge * float(TILE))))
        else:
            dec.append((jnp.exp(lge * (TILE - idx)), jnp.exp(lge * idx), jnp.exp(lge * float(TILE))))

    def sweep(direction, bb, r_ref, s_ref):
        blk = r_ref[bb]
        q = blk[:, 0:RET_W] * (HEAD_DIM ** -0.5)
        k = blk[:, RET_W:2 * RET_W]
        v = blk[:, 2 * RET_W:3 * RET_W]
        q_dec, k_dec, c_dec = dec[direction]
        s = s_ref[bb]
        cross = _dot((q * q_dec).astype(BF16), s.astype(BF16))
        kt = (k * k_dec).T
        upd = _dot(kt.astype(BF16), v.astype(BF16))
        s_ref[bb] = s * c_dec + jnp.where(bd, upd, 0.0)
        return cross, q, k, v

    for bb in range(NB_WIDE):
        o, q, k, v = sweep(0, bb, rf_ref, sf_ref)
        kb = k.astype(BF16)
        for h in range(RET_HEADS):
            sc = _dot_nt(jnp.where(masks[h], q, 0.0).astype(BF16), kb)
            o = o + _dot((sc * dm[h]).astype(BF16), jnp.where(masks[h], v, 0.0).astype(BF16))
        of_ref[bb] = o
        ob_ref[bb] = sweep(1, bb, rb_ref, sb_ref)[0]


def _ret_call(ret, log_gamma, n_lat):
    b, tt, _ = ret.shape
    nt = tt // TILE
    lge = jnp.repeat(log_gamma, HEAD_DIM, axis=-1)

    def f_idx(bi, c):
        return (bi, jnp.where(c == 0, n_lat, c - 1), 0)

    def b_idx(bi, c):
        return (bi, jnp.where(c == 0, n_lat, n_lat - c), 0)

    return pl.pallas_call(
        _ret_kernel,
        grid=(b // NB_WIDE, nt),
        in_specs=[pl.BlockSpec(memory_space=pltpu.SMEM),
                  pl.BlockSpec((2, RET_W), lambda bi, c: (0, 0)),
                  pl.BlockSpec((NB_WIDE, TILE, 3 * RET_W), f_idx),
                  pl.BlockSpec((NB_WIDE, TILE, 3 * RET_W), b_idx)],
        out_specs=[pl.BlockSpec((NB_WIDE, TILE, RET_W), f_idx), pl.BlockSpec((NB_WIDE, TILE, RET_W), b_idx)],
        out_shape=[jax.ShapeDtypeStruct((b, tt, RET_W), F32)] * 2,
        scratch_shapes=[pltpu.VMEM((NB_WIDE, RET_W, RET_W), F32), pltpu.VMEM((NB_WIDE, RET_W, RET_W), F32)],
        compiler_params=_cparams(2),
        name="retention",
    )(log_gamma, lge, ret, ret)


SWA_BLOCKS_PER_STEP = 2


def _swa_kernel(n_lat_blk, sink_ref, q_ref, kv_ref, y_ref):
    for bb in range(NB_ROWS):
        for i in range(SWA_BLOCKS_PER_STEP):
            _swa_block(n_lat_blk, sink_ref, q_ref.at[bb:bb + 1], kv_ref.at[bb:bb + 1], y_ref.at[bb:bb + 1],
                       pl.program_id(1) * SWA_BLOCKS_PER_STEP + i, slice(i * Q_BLOCK, (i + 1) * Q_BLOCK))


def _swa_block(n_lat_blk, sink_ref, q_ref, kv_ref, y_ref, j, q_rows_in_step):
    is_lat = j < n_lat_blk
    jc = jnp.minimum(j, n_lat_blk - 1)
    jp = jnp.maximum(jc - 1, 0)
    jn = jnp.minimum(jc + 1, n_lat_blk - 1)
    lat_rows = n_lat_blk * Q_BLOCK

    def piece(blk):
        return kv_ref[0, pl.ds(pl.multiple_of(blk * Q_BLOCK, Q_BLOCK), Q_BLOCK), :]

    kv_all = jnp.concatenate([piece(jp), piece(jc), piece(jn), kv_ref[0, lat_rows:, :]], axis=0)
    k_all = kv_all[:, 0:SWA_KV_W].astype(BF16)
    v_all = kv_all[:, SWA_KV_W:2 * SWA_KV_W].astype(BF16)
    n_piece = k_all.shape[0] // Q_BLOCK

    group = SWA_Q_HEADS // SWA_KV_HEADS
    rows = group * Q_BLOCK
    ql = lax.broadcasted_iota(jnp.int32, (rows, Q_BLOCK), 0) % Q_BLOCK
    kl = lax.broadcasted_iota(jnp.int32, (rows, Q_BLOCK), 1)
    ok_prev = (kl >= ql) & is_lat & (j >= 1)
    ok_next = (kl <= ql) & is_lat & (j <= n_lat_blk - 2)
    lane = lax.broadcasted_iota(jnp.int32, (Q_BLOCK, LANES), 1)
    low = lane < HEAD_DIM
    out_tiles = [[None, None] for _ in range(SWA_W // LANES)]

    for g in range(SWA_KV_HEADS):
        q_rows, sink_rows = [], []
        for hh in range(group):
            head = g * group + hh
            t, e = head // 2, head % 2
            qt = q_ref[0, q_rows_in_step, t * LANES:(t + 1) * LANES] * (HEAD_DIM ** -0.5)
            qm = jnp.where(low if e == 0 else ~low, qt, 0.0)
            if e != g:
                qm = pltpu.roll(qm, HEAD_DIM, 1)
            q_rows.append(qm)
            sink_rows.append(jnp.full((Q_BLOCK, 1), sink_ref[0, head], F32))
        qs = jnp.concatenate(q_rows, axis=0).astype(BF16)
        sink = jnp.concatenate(sink_rows, axis=0)
        s = _dot_nt(qs, k_all)

        def masked(i):
            s_i = s[:, i * Q_BLOCK:(i + 1) * Q_BLOCK]
            if i == 0:
                return jnp.where(ok_prev, s_i, NEG_INF)
            if i == 1:
                return jnp.where(is_lat, s_i, NEG_INF)
            if i == 2:
                return jnp.where(ok_next, s_i, NEG_INF)
            return s_i

        mx = masked(0)
        for i in range(1, n_piece):
            mx = jnp.maximum(mx, masked(i))
        m = jnp.maximum(jnp.max(mx, axis=-1, keepdims=True), sink)
        pr = [jnp.exp(masked(i) - m) for i in range(n_piece)]
        acc = pr[0]
        for p_i in pr[1:]:
            acc = acc + p_i
        den = jnp.sum(acc, axis=-1, keepdims=True) + jnp.exp(sink - m)
        pb = jnp.concatenate([p_i.astype(BF16) for p_i in pr], axis=1)
        o = _dot(pb, v_all) / den
        for hh in range(group):
            head = g * group + hh
            t, e = head // 2, head % 2
            oh = o[hh * Q_BLOCK:(hh + 1) * Q_BLOCK]
            if e != g:
                oh = pltpu.roll(oh, HEAD_DIM, 1)
            out_tiles[t][e] = oh
    for t in range(SWA_W // LANES):
        y_ref[0, q_rows_in_step, t * LANES:(t + 1) * LANES] = jnp.where(low, out_tiles[t][0], out_tiles[t][1])


def _swa_call(swq, swkv, sinks, n_lat_blk, n_q_blk):
    b, tt, _ = swq.shape
    step_rows = SWA_BLOCKS_PER_STEP * Q_BLOCK
    assert n_q_blk % SWA_BLOCKS_PER_STEP == 0
    return pl.pallas_call(
        functools.partial(_swa_kernel, n_lat_blk),
        grid=(b // NB_ROWS, n_q_blk // SWA_BLOCKS_PER_STEP),
        in_specs=[pl.BlockSpec(memory_space=pltpu.SMEM),
                  pl.BlockSpec((NB_ROWS, step_rows, SWA_W), lambda bi, j: (bi, j, 0)),
                  pl.BlockSpec((NB_ROWS, tt, 2 * SWA_KV_W), lambda bi, j: (bi, 0, 0))],
        out_specs=pl.BlockSpec((NB_ROWS, step_rows, SWA_W), lambda bi, j: (bi, j, 0)),
        out_shape=jax.ShapeDtypeStruct((b, n_q_blk * Q_BLOCK, SWA_W), F32),
        compiler_params=_cparams(2),
        name="swa",
    )(sinks.reshape(1, SWA_Q_HEADS), swq, swkv)


def _gdn_prep_kernel(n_lat, x_ref, hp_ref, hn_ref, ab_ref, cw_ref, alog_ref, dtb_ref, ones_ref,
                     qkv_ref, gb_ref):
    t = pl.program_id(1)
    has_prev = ((t > 0) & (t < n_lat)).astype(F32)
    has_next = (t < n_lat - 1).astype(F32)
    ones_bd = ones_ref[...]

    def l2n(u):
        return u * lax.rsqrt(_head_sum(u * u, ones_bd) + EPS)

    for bb in range(NB_WIDE):
        x = x_ref[bb]
        row = lax.broadcasted_iota(jnp.int32, x.shape, 0)
        prev_row = hp_ref[bb, 7:8, :] * has_prev
        next_row = hn_ref[bb, 0:1, :] * has_next
        xm1 = jnp.where(row == 0, prev_row, pltpu.roll(x, 1, 0))
        xp1 = jnp.where(row == TILE - 1, next_row, pltpu.roll(x, TILE - 1, 0))
        conv = xm1 * cw_ref[0:1, :] + x * cw_ref[1:2, :] + xp1 * cw_ref[2:3, :]
        s = _silu(conv)
        qkv_ref[bb, :, 0:GDN_W] = l2n(s[:, 0:GDN_W]) * (HEAD_DIM ** -0.5)
        qkv_ref[bb, :, GDN_W:2 * GDN_W] = l2n(s[:, GDN_W:2 * GDN_W])
        qkv_ref[bb, :, 2 * GDN_W:3 * GDN_W] = s[:, 2 * GDN_W:3 * GDN_W]

        ab = ab_ref[bb]
        lane = lax.broadcasted_iota(jnp.int32, ab.shape, 1)
        g = -jnp.exp(alog_ref[...]) * jax.nn.softplus(ab + dtb_ref[...])
        beta = jax.nn.sigmoid(ab)
        gb_ref[bb] = jnp.where(lane < 2 * GDN_HEADS, g, jnp.where(lane < 4 * GDN_HEADS, beta, 0.0))


def _gdn_prep_call(gqkv, ab, conv_w, a_log, dt_bias, ones_bd, n_lat):
    b, tt, w = gqkv.shape
    nt = tt // TILE
    sub = 8
    per = TILE // sub
    last = tt // sub - 1
    pad = LANES - 2 * GDN_HEADS
    alog_pad = jnp.pad(a_log.reshape(1, 2 * GDN_HEADS), ((0, 0), (0, pad)))
    dtb_pad = jnp.pad(dt_bias.reshape(1, 2 * GDN_HEADS), ((0, 0), (0, pad)))
    return pl.pallas_call(
        functools.partial(_gdn_prep_kernel, n_lat),
        grid=(b // NB_WIDE, nt),
        in_specs=[pl.BlockSpec((NB_WIDE, TILE, w), lambda bi, t: (bi, t, 0)),
                  pl.BlockSpec((NB_WIDE, sub, w), lambda bi, t: (bi, jnp.maximum(t * per - 1, 0), 0)),
                  pl.BlockSpec((NB_WIDE, sub, w), lambda bi, t: (bi, jnp.minimum((t + 1) * per, last), 0)),
                  pl.BlockSpec((NB_WIDE, TILE, LANES), lambda bi, t: (bi, t, 0)),
                  pl.BlockSpec((CONV_W, w), lambda bi, t: (0, 0)),
                  pl.BlockSpec((1, LANES), lambda bi, t: (0, 0)),
                  pl.BlockSpec((1, LANES), lambda bi, t: (0, 0)),
                  pl.BlockSpec((GDN_W, GDN_W), lambda bi, t: (0, 0))],
        out_specs=[pl.BlockSpec((NB_WIDE, TILE, w), lambda bi, t: (bi, t, 0)),
                   pl.BlockSpec((NB_WIDE, TILE, LANES), lambda bi, t: (bi, t, 0))],
        out_shape=[jax.ShapeDtypeStruct((b, tt, w), F32), jax.ShapeDtypeStruct((b, tt, LANES), F32)],
        compiler_params=_cparams(2),
        name="gdn_prep",
    )(gqkv, gqkv, gqkv, ab, conv_w, alog_pad, dtb_pad, ones_bd)


N_SUB = TILE // GDN_CHUNK


def _gdn_kernel(n_lat, ones_ref, tri_ref, xf_ref, gbf_ref, xb_ref, gbb_ref,
                of_ref, ob_ref, sf_ref, sb_ref):
    c = pl.program_id(1)

    @pl.when(c == 0)
    def _():
        sf_ref[...] = jnp.zeros_like(sf_ref)
        sb_ref[...] = jnp.zeros_like(sb_ref)

    ones_bd = ones_ref[...]
    masks = _head_masks(TILE, GDN_W)
    sub_masks = _head_masks(GDN_CHUNK, GDN_W)
    ri = lax.broadcasted_iota(jnp.int32, (TILE, TILE), 0)
    ci = lax.broadcasted_iota(jnp.int32, (TILE, TILE), 1)
    bd = (ri // GDN_CHUNK) == (ci // GDN_CHUNK)
    x_refs = (xf_ref, xb_ref)
    gb_refs = (gbf_ref, gbb_ref)
    s_refs = (sf_ref, sb_ref)
    o_refs = (of_ref, ob_ref)
    dirs = tuple(range(2 * NB))

    def expand(x, first_col):
        e = jnp.zeros((TILE, GDN_W), F32)
        for h in range(GDN_HEADS):
            e = jnp.where(masks[h], x[:, first_col + h:first_col + h + 1], e)
        return e

    pre = []
    for d in dirs:
        blk = x_refs[d % 2][d // 2]
        qn = blk[:, 0:GDN_W]
        kn = blk[:, GDN_W:2 * GDN_W]
        v = blk[:, 2 * GDN_W:3 * GDN_W]
        gbv = gb_refs[d % 2][d // 2]
        gc = _dot_sel_x(tri_ref[d % 2], gbv)
        gtot = _dot_sel_x(ones_bd, gbv)
        gc_e = expand(gc, (d % 2) * GDN_HEADS)
        gtot_e = expand(gtot, (d % 2) * GDN_HEADS)
        beta_e = expand(gbv, 2 * GDN_HEADS + (d % 2) * GDN_HEADS)
        egc = jnp.exp(gc_e)
        pre.append(dict(qn=qn, kn=kn, kn_b=kn.astype(BF16), gb_t=gbv.T, gc=gc, gc_t=gc.T,
                        vb_t=(v * beta_e).T, kb_t=(kn * beta_e * egc).T, qd=qn * egc,
                        kt=kn * jnp.exp(gtot_e - gc_e), gtot_dec=jnp.exp(gtot_e),
                        causal=bd & ((ri >= ci) if d % 2 == 0 else (ri <= ci)),
                        causal_t=bd & ((ri <= ci) if d % 2 == 0 else (ri >= ci))))

    chains = [(d, h) for h in range(GDN_HEADS) for d in dirs]
    m_t, z_acc, attn = {}, {}, {}
    for (d, h) in chains:
        p = pre[d]
        r = (d % 2) * GDN_HEADS + h
        kk = _dot_nt(jnp.where(masks[h], p["kn"], 0.0).astype(BF16), p["kn_b"])
        qk = _dot_nt(jnp.where(masks[h], p["qn"], 0.0).astype(BF16), p["kn_b"])
        diff = p["gc"][:, r:r + 1] - p["gc_t"][r:r + 1, :]
        attn[(d, h)] = qk * jnp.exp(jnp.where(p["causal"], diff, NEG_INF))
        brow = p["gb_t"][2 * GDN_HEADS + r:2 * GDN_HEADS + r + 1, :]
        m_t[(d, h)] = jnp.where(ri == ci, 0.0,
                                kk * brow * jnp.exp(jnp.where(p["causal_t"], -diff, NEG_INF)))
        z_acc[(d, h)] = jnp.concatenate([p["vb_t"][h * HEAD_DIM:(h + 1) * HEAD_DIM],
                                         p["kb_t"][h * HEAD_DIM:(h + 1) * HEAD_DIM]], axis=0)

    def off_diag(s_blk):
        return ((ri // (2 * s_blk)) == (ci // (2 * s_blk))) & ((ri // s_blk) != (ci // s_blk))

    x_inv = {}
    for ch in chains:
        x_inv[ch] = (ri == ci).astype(F32) - jnp.where(off_diag(1), m_t[ch], 0.0)
    s_blk = 2
    while s_blk < GDN_CHUNK // 2:
        for ch in chains:
            xb = x_inv[ch].astype(BF16)
            pc = _dot(xb, jnp.where(off_diag(s_blk), m_t[ch], 0.0).astype(BF16))
            x_inv[ch] = x_inv[ch] - _dot(pc.astype(BF16), xb)
        s_blk *= 2
    for ch in chains:
        xb = x_inv[ch].astype(BF16)
        y = _dot(z_acc[ch].astype(BF16), xb)
        yc = _dot(y.astype(BF16), jnp.where(off_diag(s_blk), m_t[ch], 0.0).astype(BF16))
        z_acc[ch] = y - _dot(yc.astype(BF16), xb)

    u = [jnp.concatenate([z_acc[(d, h)][0:HEAD_DIM] for h in range(GDN_HEADS)], axis=0).T for d in dirs]
    w = [jnp.concatenate([z_acc[(d, h)][HEAD_DIM:2 * HEAD_DIM] for h in range(GDN_HEADS)], axis=0).T
         for d in dirs]

    state_bd = (ri // HEAD_DIM) == (ci // HEAD_DIM)
    kt_t = [pre[d]["kt"].T for d in dirs]
    s = [s_refs[d % 2][d // 2] for d in dirs]
    outs = [[None] * N_SUB for _ in dirs]
    for i in range(N_SUB):
        for d in dirs:
            sub = i if d % 2 == 0 else N_SUB - 1 - i
            r0 = sub * GDN_CHUNK
            wq = jnp.concatenate([w[d][r0:r0 + GDN_CHUNK], pre[d]["qd"][r0:r0 + GDN_CHUNK]], axis=0)
            ws = _dot(wq.astype(BF16), s[d].astype(BF16))
            v_new = u[d][r0:r0 + GDN_CHUNK] - ws[0:GDN_CHUNK]
            o_sub = ws[GDN_CHUNK:2 * GDN_CHUNK]
            for h in range(GDN_HEADS):
                a_blk = attn[(d, h)][r0:r0 + GDN_CHUNK, r0:r0 + GDN_CHUNK].astype(BF16)
                o_sub = o_sub + _dot(a_blk, jnp.where(sub_masks[h], v_new, 0.0).astype(BF16))
            outs[d][sub] = o_sub
            upd = _dot(kt_t[d][:, r0:r0 + GDN_CHUNK].astype(BF16), v_new.astype(BF16))
            s[d] = s[d] * pre[d]["gtot_dec"][r0:r0 + 1, :] + jnp.where(state_bd, upd, 0.0)
    for d in dirs:
        s_refs[d % 2][d // 2] = s[d]
        o_refs[d % 2][d // 2] = jnp.concatenate(outs[d], axis=0)


def _gdn_consts():
    i = np.arange(TILE)
    same = (i[:, None] // GDN_CHUNK) == (i[None, :] // GDN_CHUNK)
    tri = np.stack([same & (i[None, :] <= i[:, None]), same & (i[None, :] >= i[:, None])])
    return jnp.asarray(same, BF16), jnp.asarray(tri, BF16)


def _gdn_call(qkvn, gb, consts, n_lat):
    b, tt, w = qkvn.shape
    nt = tt // TILE
    ones_chunk, tri = consts

    def f_idx(bi, c):
        return (bi, jnp.where(c == 0, n_lat, c - 1), 0)

    def b_idx(bi, c):
        return (bi, jnp.where(c == 0, n_lat, n_lat - c), 0)

    return pl.pallas_call(
        functools.partial(_gdn_kernel, n_lat),
        grid=(b // NB, nt),
        in_specs=[pl.BlockSpec((TILE, TILE), lambda bi, c: (0, 0)),
                  pl.BlockSpec((2, TILE, TILE), lambda bi, c: (0, 0, 0)),
                  pl.BlockSpec((NB, TILE, w), f_idx),
                  pl.BlockSpec((NB, TILE, LANES), f_idx),
                  pl.BlockSpec((NB, TILE, w), b_idx),
                  pl.BlockSpec((NB, TILE, LANES), b_idx)],
        out_specs=[pl.BlockSpec((NB, TILE, GDN_W), f_idx), pl.BlockSpec((NB, TILE, GDN_W), b_idx)],
        out_shape=[jax.ShapeDtypeStruct((b, tt, GDN_W), F32)] * 2,
        scratch_shapes=[pltpu.VMEM((NB, GDN_W, GDN_W), F32), pltpu.VMEM((NB, GDN_W, GDN_W), F32)],
        compiler_params=_cparams(2),
        name="gdn",
    )(ones_chunk, tri, qkvn, gb, qkvn, gb)


FF_CHUNK = 512


def _post_kernel(d_ff, final, n_lat_tiles, with_ctx, x_ref, *refs):
    if with_ctx:
        c_ref, refs = refs[0], refs[1:]
        is_lat = pl.program_id(1) < n_lat_tiles
        xs = [jnp.where(is_lat, x_ref[bb], c_ref[bb]) for bb in range(NB)]
    else:
        xs = [x_ref[bb] for bb in range(NB)]
    (rf_ref, rb_ref, rg_ref, ys_ref, gf_ref, gb_ref, gz_ref, g1_ref, sh_ref, sc_ref, g2_ref, nw_ref,
     fw_ref, rnw_ref, gnw_ref, ones_ref, wo_ref, w1_ref, w2_ref, o_ref) = refs
    ones_head = ones_ref[...]

    def rows(per_sample):
        return jnp.concatenate([per_sample(bb) for bb in range(NB)], axis=0)

    def head_out(f_ref, b_ref, gate_ref, w_ref):
        def one(bb):
            o = f_ref[bb] + b_ref[bb]
            ms = _head_sum(o * o, ones_head) * (1.0 / HEAD_DIM)
            return (o * lax.rsqrt(ms + EPS) * w_ref[...] * _silu(gate_ref[bb])).astype(BF16)
        return rows(one)

    mix = (_dot(head_out(rf_ref, rb_ref, rg_ref, rnw_ref), wo_ref[0:RET_W, :])
           + _dot(rows(lambda bb: ys_ref[bb].astype(BF16)), wo_ref[RET_W:RET_W + SWA_W, :])
           + _dot(head_out(gf_ref, gb_ref, gz_ref, gnw_ref), wo_ref[RET_W + SWA_W:, :]))
    x1 = [xs[bb] + g1_ref[bb] * mix[bb * TILE:(bb + 1) * TILE] for bb in range(NB)]

    def ffn_in(bb):
        y = x1[bb] * lax.rsqrt(jnp.mean(x1[bb] * x1[bb], axis=-1, keepdims=True) + EPS) * nw_ref[...]
        return (y * (1.0 + sc_ref[bb]) + sh_ref[bb]).astype(BF16)

    h = rows(ffn_in)
    acc = jnp.zeros((NB * TILE, x1[0].shape[1]), F32)
    for c0 in range(0, d_ff, FF_CHUNK):
        cw = min(FF_CHUNK, d_ff - c0)
        gate = _dot(h, w1_ref[:, c0:c0 + cw])
        up = _dot(h, w1_ref[:, d_ff + c0:d_ff + c0 + cw])
        acc = acc + _dot((_silu(gate) * up).astype(BF16), w2_ref[c0:c0 + cw, :])
    for bb in range(NB):
        x2 = x1[bb] + g2_ref[bb] * acc[bb * TILE:(bb + 1) * TILE]
        if final:
            x2 = x2 * lax.rsqrt(jnp.mean(x2 * x2, axis=-1, keepdims=True) + EPS) * fw_ref[...]
        o_ref[bb] = x2


def _post_call(x_lat, x_ctx, ctx_tile, o_rf, o_rb, ret, y_s, o_gf, o_gb, gz, mod, norm_w, final_w,
               ret_norm_w, gdn_norm_w, ones_head, wo_bf, w1_bf, w2_bf, n_lat_tiles, ctx_row, final):
    b, _, d = x_lat.shape
    d_ff = w2_bf.shape[0]
    with_ctx = x_ctx is not None
    n_steps = n_lat_tiles + 1 if with_ctx else n_lat_tiles

    def mod_spec(k):
        return _mod_spec(d, k, n_lat_tiles if with_ctx else None, ctx_row, NB)

    def tile_spec(w):
        return pl.BlockSpec((NB, TILE, w), lambda bi, t: (bi, t, 0))

    full = lambda shape: pl.BlockSpec(shape, lambda bi, t: (0,) * len(shape))
    x_specs = _row_source_specs(d, n_lat_tiles, ctx_tile, NB) if with_ctx else [tile_spec(d)]
    x_args = (x_lat, x_ctx) if with_ctx else (x_lat,)
    return pl.pallas_call(
        functools.partial(_post_kernel, d_ff, final, n_lat_tiles, with_ctx),
        grid=(b // NB, n_steps),
        in_specs=x_specs + [
                  tile_spec(RET_W), tile_spec(RET_W),
                  pl.BlockSpec((NB, TILE, RET_W), lambda bi, t: (bi, t, 3)),
                  tile_spec(SWA_W), tile_spec(GDN_W), tile_spec(GDN_W), tile_spec(GDN_W),
                  mod_spec(2), mod_spec(3), mod_spec(4), mod_spec(5),
                  full((1, d)), full((1, d)), full((1, RET_W)), full((1, GDN_W)), full(ones_head.shape),
                  full(wo_bf.shape), full(w1_bf.shape), full(w2_bf.shape)],
        out_specs=tile_spec(d),
        out_shape=jax.ShapeDtypeStruct((b, n_steps * TILE, d), F32),
        compiler_params=_cparams(2),
        name="out_proj_ffn",
    )(*x_args, o_rf, o_rb, ret, y_s, o_gf, o_gb, gz, mod, mod, mod, mod, norm_w.reshape(1, d),
      final_w.reshape(1, d), ret_norm_w.reshape(1, RET_W),
      jnp.tile(gdn_norm_w, GDN_HEADS).reshape(1, GDN_W), ones_head, wo_bf, w1_bf, w2_bf)


def _rope_tables(n_lat_rows, n_ctx_rows):
    pos = np.arange(n_lat_rows)
    lane = np.arange(LANES) % HEAD_DIM
    axis = lane // (2 * N_FREQ)
    second_half = (lane % (2 * N_FREQ)) >= N_FREQ
    freqs = jnp.asarray(ROPE_BASE, F32) ** (-jnp.arange(N_FREQ, dtype=F32) / N_FREQ)
    coord = jnp.where(jnp.asarray(axis == 0)[None, :],
                      jnp.asarray(pos // GRID_W, F32)[:, None],
                      jnp.asarray(pos % GRID_W, F32)[:, None])
    ang = coord * freqs[jnp.asarray(lane % N_FREQ)][None, :]
    cos = jnp.cos(ang)
    sin = jnp.where(jnp.asarray(second_half)[None, :], jnp.sin(ang), -jnp.sin(ang))
    cos = jnp.concatenate([cos, jnp.ones((n_ctx_rows, LANES), F32)], axis=0)
    sin = jnp.concatenate([sin, jnp.zeros((n_ctx_rows, LANES), F32)], axis=0)
    return cos, sin


def kernel(x, c, ctx, c_ctx, ada_w, ada_b, norm_mix_w, norm_ffn_w, w_in, ret_rate, ret_norm_w,
           swa_sinks, gdn_conv_w, gdn_a_log, gdn_dt_bias, gdn_norm_w, w_out, w_ffn_in, w_ffn_out,
           final_norm_w):
    b, l, d = x.shape
    lc = ctx.shape[1]
    depth = ada_w.shape[0]
    assert l % TILE == 0 and lc == TILE and d % LANES == 0
    assert b % NB == 0 and b % NB_ROWS == 0 and b % NB_WIDE == 0 and NB_ROWS % NB == 0 and b + NB_ROWS <= MOD_ROWS
    n_lat = l // TILE

    x_lat, x_ctx, ctx_tile = x, ctx, 0
    c_pad = jnp.zeros((MOD_ROWS, d), F32).at[:b].set(c).at[b:b + NB_ROWS].set(c_ctx)
    mod = _ada_call(c_pad, ada_w, ada_b)
    cos_t, sin_t = _rope_tables(l, lc)
    consts = _gdn_consts()
    i = np.arange(RET_W)
    ones_head = jnp.asarray((i[:, None] // HEAD_DIM) == (i[None, :] // HEAD_DIM), BF16)
    log_gamma = jnp.log1p(-jnp.exp2(-ret_rate.astype(F32)))

    for layer in range(depth):
        last = layer == depth - 1
        mod_l = mod[layer].reshape(MOD_ROWS, 1, N_MOD * d)
        w_in_bf = jnp.pad(w_in[layer], ((0, 0), (0, IN_W_PAD - IN_W))).astype(BF16)
        ret, swq, swkv, gqkv, gz, ab = _inproj_call(x_lat, x_ctx, ctx_tile, mod_l, norm_mix_w[layer],
                                                    w_in_bf, cos_t, sin_t, n_lat, b)
        o_rf, o_rb = _ret_call(ret, log_gamma[layer], n_lat)
        n_lat_blk = l // Q_BLOCK
        n_q_blk = n_lat_blk if last else n_lat_blk + lc // Q_BLOCK
        y_s = _swa_call(swq, swkv, swa_sinks[layer], n_lat_blk, n_q_blk)
        qkvn, gb = _gdn_prep_call(gqkv, ab, gdn_conv_w[layer], gdn_a_log[layer], gdn_dt_bias[layer],
                                  ones_head, n_lat)
        o_gf, o_gb = _gdn_call(qkvn, gb, consts, n_lat)
        x_new = _post_call(x_lat, None if last else x_ctx, ctx_tile, o_rf, o_rb, ret, y_s, o_gf, o_gb, gz,
                           mod_l, norm_ffn_w[layer], final_norm_w, ret_norm_w[layer], gdn_norm_w[layer],
                           ones_head, w_out[layer].astype(BF16), w_ffn_in[layer].astype(BF16),
                           w_ffn_out[layer].astype(BF16), n_lat, b, last)
        x_lat, x_ctx, ctx_tile = x_new, x_new, n_lat
    return x_lat
```

```python
import functools

import numpy as np
import jax
import jax.numpy as jnp
from jax import lax
from jax.experimental import pallas as pl
from jax.experimental.pallas import tpu as pltpu

F32 = jnp.float32
BF16 = jnp.bfloat16

HEAD_DIM = 64
N_FREQ = HEAD_DIM // 4
GRID_W = 64
RET_HEADS = 4
SWA_Q_HEADS = 8
SWA_KV_HEADS = 2
GDN_HEADS = 4
RET_W = RET_HEADS * HEAD_DIM
SWA_W = SWA_Q_HEADS * HEAD_DIM
SWA_KV_W = SWA_KV_HEADS * HEAD_DIM
GDN_W = GDN_HEADS * HEAD_DIM
WINDOW = 128
Q_BLOCK = 128
GDN_CHUNK = 64
CONV_W = 3
ROPE_BASE = 10000.0
EPS = 1e-6
NEG_INF = -1e30
N_MOD = 6

LANES = 128
TILE = 256
MOD_ROWS = 16
NB = 2
NB_ROWS = 4
NB_WIDE = 8
VMEM_LIMIT = 56 * 1024 * 1024

COL_RET = 0
COL_SWA_Q = 4 * RET_W
COL_SWA_KV = COL_SWA_Q + SWA_W
COL_GDN_QKV = COL_SWA_KV + 2 * SWA_KV_W
COL_GDN_Z = COL_GDN_QKV + 3 * GDN_W
COL_AB = COL_GDN_Z + GDN_W
IN_W = COL_AB + 4 * GDN_HEADS
IN_W_PAD = COL_AB + LANES


def _dot(a, b):
    return jnp.dot(a, b, preferred_element_type=F32)


def _dot_nt(a, b):
    return lax.dot_general(a, b, (((1,), (1,)), ((), ())), preferred_element_type=F32)


def _split3(x):
    hi = x.astype(BF16)
    r = x - hi.astype(F32)
    mid = r.astype(BF16)
    lo = (r - mid.astype(F32)).astype(BF16)
    return hi, mid, lo


def _dot_x_sel(x, sel):
    hi, mid, lo = _split3(x)
    return _dot(hi, sel) + _dot(mid, sel) + _dot(lo, sel)


def _dot_sel_x(sel, x):
    hi, mid, lo = _split3(x)
    return _dot(sel, hi) + _dot(sel, mid) + _dot(sel, lo)


def _silu(x):
    return x * jax.nn.sigmoid(x)


def _head_masks(rows, width):
    lane = lax.broadcasted_iota(jnp.int32, (rows, width), 1)
    return [(lane // HEAD_DIM) == h for h in range(width // HEAD_DIM)]


def _head_sum(x, ones_bd):
    return _dot_x_sel(x, ones_bd)


def _cparams(n_axes):
    return pltpu.CompilerParams(dimension_semantics=("arbitrary",) * n_axes,
                                vmem_limit_bytes=VMEM_LIMIT)


def _ada_kernel(c_ref, w_ref, b_ref, o_ref):
    cs = _silu(c_ref[...])
    o_ref[0] = _dot(cs.astype(BF16), w_ref[0].astype(BF16)) + b_ref[0]


def _ada_call(c_pad, ada_w, ada_b):
    depth, d, n = ada_w.shape
    tn = 1536
    return pl.pallas_call(
        _ada_kernel,
        grid=(depth, n // tn),
        in_specs=[pl.BlockSpec((MOD_ROWS, d), lambda l, j: (0, 0)),
                  pl.BlockSpec((1, d, tn), lambda l, j: (l, 0, j)),
                  pl.BlockSpec((1, 1, tn), lambda l, j: (l, 0, j))],
        out_specs=pl.BlockSpec((1, MOD_ROWS, tn), lambda l, j: (l, 0, j)),
        out_shape=jax.ShapeDtypeStruct((depth, MOD_ROWS, n), F32),
        compiler_params=_cparams(2),
        name="ada_mod",
    )(c_pad, ada_w, ada_b.reshape(depth, 1, n))


def _rope(t, cos, sin_signed):
    lane = lax.broadcasted_iota(jnp.int32, t.shape, 1)
    first_half = (lane % (2 * N_FREQ)) < N_FREQ
    partner = jnp.where(first_half, pltpu.roll(t, LANES - N_FREQ, 1), pltpu.roll(t, N_FREQ, 1))
    return t * cos + partner * sin_signed


def _inproj_kernel(n_lat_tiles, x_ref, c_ref, sh_ref, sc_ref, nw_ref, w_ref, cos_ref, sin_ref,
                   ret_ref, swq_ref, swkv_ref, gqkv_ref, gz_ref, ab_ref):
    is_lat = pl.program_id(1) < n_lat_tiles
    hs = []
    for bb in range(NB_ROWS):
        x = jnp.where(is_lat, x_ref[bb], c_ref[bb])
        y = x * lax.rsqrt(jnp.mean(x * x, axis=-1, keepdims=True) + EPS) * nw_ref[...]
        hs.append((y * (1.0 + sc_ref[bb]) + sh_ref[bb]).astype(BF16))
    h = jnp.concatenate(hs, axis=0)
    cos = cos_ref[...]
    sin = sin_ref[...]

    def proj(col, width, out_ref, out_col, roped_lanes):
        r = _dot(h, w_ref[:, col:col + width])
        for bb in range(NB_ROWS):
            for g in range(width // LANES):
                rg = r[bb * TILE:(bb + 1) * TILE, g * LANES:(g + 1) * LANES]
                if g * LANES < roped_lanes:
                    rg = _rope(rg, cos, sin)
                out_ref[bb, :, out_col + g * LANES:out_col + (g + 1) * LANES] = rg

    for g in range(4):
        proj(COL_RET + g * RET_W, RET_W, ret_ref, g * RET_W, RET_W if g < 2 else 0)
    for g in range(SWA_W // TILE):
        proj(COL_SWA_Q + g * TILE, TILE, swq_ref, g * TILE, TILE)
    proj(COL_SWA_KV, 2 * SWA_KV_W, swkv_ref, 0, SWA_KV_W)
    for g in range(3):
        proj(COL_GDN_QKV + g * GDN_W, GDN_W, gqkv_ref, g * GDN_W, 0)
    proj(COL_GDN_Z, GDN_W, gz_ref, 0, 0)
    proj(COL_AB, LANES, ab_ref, 0, 0)


def _row_source_specs(d, n_lat_tiles, ctx_tile, nb):
    return [pl.BlockSpec((nb, TILE, d), lambda bi, t: (bi, jnp.minimum(t, n_lat_tiles - 1), 0)),
            pl.BlockSpec((nb, TILE, d), lambda bi, t: (bi, ctx_tile, 0))]


def _mod_spec(d, k, n_lat_tiles, ctx_row, nb):
    if n_lat_tiles is None:
        return pl.BlockSpec((nb, 1, d), lambda bi, t: (bi, 0, k))
    return pl.BlockSpec((nb, 1, d), lambda bi, t: (jnp.where(t >= n_lat_tiles, ctx_row // nb, bi), 0, k))


def _inproj_call(x_lat, x_ctx, ctx_tile, mod, norm_w, w_in_bf, cos_t, sin_t, n_lat_tiles, ctx_row):
    b, _, d = x_lat.shape
    nt = n_lat_tiles + 1
    tt = nt * TILE
    widths = (4 * RET_W, SWA_W, 2 * SWA_KV_W, 3 * GDN_W, GDN_W, LANES)
    return pl.pallas_call(
        functools.partial(_inproj_kernel, n_lat_tiles),
        grid=(b // NB_ROWS, nt),
        in_specs=_row_source_specs(d, n_lat_tiles, ctx_tile, NB_ROWS) + [
                  _mod_spec(d, 0, n_lat_tiles, ctx_row, NB_ROWS),
                  _mod_spec(d, 1, n_lat_tiles, ctx_row, NB_ROWS),
                  pl.BlockSpec((1, d), lambda bi, t: (0, 0)),
                  pl.BlockSpec((d, IN_W_PAD), lambda bi, t: (0, 0)),
                  pl.BlockSpec((TILE, LANES), lambda bi, t: (t, 0)),
                  pl.BlockSpec((TILE, LANES), lambda bi, t: (t, 0))],
        out_specs=[pl.BlockSpec((NB_ROWS, TILE, w), lambda bi, t: (bi, t, 0)) for w in widths],
        out_shape=[jax.ShapeDtypeStruct((b, tt, w), F32) for w in widths],
        compiler_params=_cparams(2),
        name="in_proj",
    )(x_lat, x_ctx, mod, mod, norm_w.reshape(1, d), w_in_bf, cos_t, sin_t)


def _ret_kernel(lg_ref, lge_ref, rf_ref, rb_ref, of_ref, ob_ref, sf_ref, sb_ref):
    c = pl.program_id(1)

    @pl.when(c == 0)
    def _():
        sf_ref[...] = jnp.zeros_like(sf_ref)
        sb_ref[...] = jnp.zeros_like(sb_ref)

    masks = _head_masks(TILE, RET_W)
    ri = lax.broadcasted_iota(jnp.int32, (TILE, TILE), 0)
    ci = lax.broadcasted_iota(jnp.int32, (TILE, TILE), 1)
    bd = (ri // HEAD_DIM) == (ci // HEAD_DIM)
    idx = ri.astype(F32)
    diff = (ri - ci).astype(F32)
    dm = [jnp.where(diff >= 0, jnp.exp(lg_ref[0, h] * jnp.maximum(diff, 0.0)), 0.0)
          + jnp.where(diff <= 0, jnp.exp(lg_ref[1, h] * jnp.maximum(-diff, 0.0)), 0.0)
          for h in range(RET_HEADS)]
    dec = []
    for direction in (0, 1):
        lge = lge_ref[direction:direction + 1, :]
        if direction == 0:
            dec.append((jnp.exp(lge * (idx + 1.0)), jnp.exp(lge * (TILE - 1.0 - idx)),
                        jnp.exp(lge * float(TILE))))
        else:
            dec.append((jnp.exp(lge * (TILE - idx)), jnp.exp(lge * idx), jnp.exp(lge * float(TILE))))

    def sweep(direction, bb, r_ref, s_ref):
        blk = r_ref[bb]
        q = blk[:, 0:RET_W] * (HEAD_DIM ** -0.5)
        k = blk[:, RET_W:2 * RET_W]
        v = blk[:, 2 * RET_W:3 * RET_W]
        q_dec, k_dec, c_dec = dec[direction]
        s = s_ref[bb]
        cross = _dot((q * q_dec).astype(BF16), s.astype(BF16))
        kt = (k * k_dec).T
        upd = _dot(kt.astype(BF16), v.astype(BF16))
        s_ref[bb] = s * c_dec + jnp.where(bd, upd, 0.0)
        return cross, q, k, v

    for bb in range(NB_WIDE):
        o, q, k, v = sweep(0, bb, rf_ref, sf_ref)
        kb = k.astype(BF16)
        for h in range(RET_HEADS):
            sc = _dot_nt(jnp.where(masks[h], q, 0.0).astype(BF16), kb)
            o = o + _dot((sc * dm[h]).astype(BF16), jnp.where(masks[h], v, 0.0).astype(BF16))
        of_ref[bb] = o
        ob_ref[bb] = sweep(1, bb, rb_ref, sb_ref)[0]


def _ret_call(ret, log_gamma, n_lat):
    b, tt, _ = ret.shape
    nt = tt // TILE
    lge = jnp.repeat(log_gamma, HEAD_DIM, axis=-1)

    def f_idx(bi, c):
        return (bi, jnp.where(c == 0, n_lat, c - 1), 0)

    def b_idx(bi, c):
        return (bi, jnp.where(c == 0, n_lat, n_lat - c), 0)

    return pl.pallas_call(
        _ret_kernel,
        grid=(b // NB_WIDE, nt),
        in_specs=[pl.BlockSpec(memory_space=pltpu.SMEM),
                  pl.BlockSpec((2, RET_W), lambda bi, c: (0, 0)),
                  pl.BlockSpec((NB_WIDE, TILE, 3 * RET_W), f_idx),
                  pl.BlockSpec((NB_WIDE, TILE, 3 * RET_W), b_idx)],
        out_specs=[pl.BlockSpec((NB_WIDE, TILE, RET_W), f_idx), pl.BlockSpec((NB_WIDE, TILE, RET_W), b_idx)],
        out_shape=[jax.ShapeDtypeStruct((b, tt, RET_W), F32)] * 2,
        scratch_shapes=[pltpu.VMEM((NB_WIDE, RET_W, RET_W), F32), pltpu.VMEM((NB_WIDE, RET_W, RET_W), F32)],
        compiler_params=_cparams(2),
        name="retention",
    )(log_gamma, lge, ret, ret)


SWA_BLOCKS_PER_STEP = 2


def _swa_kernel(n_lat_blk, sink_ref, q_ref, kv_ref, y_ref):
    for bb in range(NB_ROWS):
        for i in range(SWA_BLOCKS_PER_STEP):
            _swa_block(n_lat_blk, sink_ref, q_ref.at[bb:bb + 1], kv_ref.at[bb:bb + 1], y_ref.at[bb:bb + 1],
                       pl.program_id(1) * SWA_BLOCKS_PER_STEP + i, slice(i * Q_BLOCK, (i + 1) * Q_BLOCK))


def _swa_block(n_lat_blk, sink_ref, q_ref, kv_ref, y_ref, j, q_rows_in_step):
    is_lat = j < n_lat_blk
    jc = jnp.minimum(j, n_lat_blk - 1)
    jp = jnp.maximum(jc - 1, 0)
    jn = jnp.minimum(jc + 1, n_lat_blk - 1)
    lat_rows = n_lat_blk * Q_BLOCK

    def piece(blk):
        return kv_ref[0, pl.ds(pl.multiple_of(blk * Q_BLOCK, Q_BLOCK), Q_BLOCK), :]

    kv_all = jnp.concatenate([piece(jp), piece(jc), piece(jn), kv_ref[0, lat_rows:, :]], axis=0)
    k_all = kv_all[:, 0:SWA_KV_W].astype(BF16)
    v_all = kv_all[:, SWA_KV_W:2 * SWA_KV_W].astype(BF16)
    n_piece = k_all.shape[0] // Q_BLOCK

    group = SWA_Q_HEADS // SWA_KV_HEADS
    rows = group * Q_BLOCK
    ql = lax.broadcasted_iota(jnp.int32, (rows, Q_BLOCK), 0) % Q_BLOCK
    kl = lax.broadcasted_iota(jnp.int32, (rows, Q_BLOCK), 1)
    ok_prev = (kl >= ql) & is_lat & (j >= 1)
    ok_next = (kl <= ql) & is_lat & (j <= n_lat_blk - 2)
    lane = lax.broadcasted_iota(jnp.int32, (Q_BLOCK, LANES), 1)
    low = lane < HEAD_DIM
    out_tiles = [[None, None] for _ in range(SWA_W // LANES)]

    for g in range(SWA_KV_HEADS):
        q_rows, sink_rows = [], []
        for hh in range(group):
            head = g * group + hh
            t, e = head // 2, head % 2
            qt = q_ref[0, q_rows_in_step, t * LANES:(t + 1) * LANES] * (HEAD_DIM ** -0.5)
            qm = jnp.where(low if e == 0 else ~low, qt, 0.0)
            if e != g:
                qm = pltpu.roll(qm, HEAD_DIM, 1)
            q_rows.append(qm)
            sink_rows.append(jnp.full((Q_BLOCK, 1), sink_ref[0, head], F32))
        qs = jnp.concatenate(q_rows, axis=0).astype(BF16)
        sink = jnp.concatenate(sink_rows, axis=0)
        s = _dot_nt(qs, k_all)

        def masked(i):
            s_i = s[:, i * Q_BLOCK:(i + 1) * Q_BLOCK]
            if i == 0:
                return jnp.where(ok_prev, s_i, NEG_INF)
            if i == 1:
                return jnp.where(is_lat, s_i, NEG_INF)
            if i == 2:
                return jnp.where(ok_next, s_i, NEG_INF)
            return s_i

        mx = masked(0)
        for i in range(1, n_piece):
            mx = jnp.maximum(mx, masked(i))
        m = jnp.maximum(jnp.max(mx, axis=-1, keepdims=True), sink)
        pr = [jnp.exp(masked(i) - m) for i in range(n_piece)]
        acc = pr[0]
        for p_i in pr[1:]:
            acc = acc + p_i
        den = jnp.sum(acc, axis=-1, keepdims=True) + jnp.exp(sink - m)
        pb = jnp.concatenate([p_i.astype(BF16) for p_i in pr], axis=1)
        o = _dot(pb, v_all) / den
        for hh in range(group):
            head = g * group + hh
            t, e = head // 2, head % 2
            oh = o[hh * Q_BLOCK:(hh + 1) * Q_BLOCK]
            if e != g:
                oh = pltpu.roll(oh, HEAD_DIM, 1)
            out_tiles[t][e] = oh
    for t in range(SWA_W // LANES):
        y_ref[0, q_rows_in_step, t * LANES:(t + 1) * LANES] = jnp.where(low, out_tiles[t][0], out_tiles[t][1])


def _swa_call(swq, swkv, sinks, n_lat_blk, n_q_blk):
    b, tt, _ = swq.shape
    step_rows = SWA_BLOCKS_PER_STEP * Q_BLOCK
    assert n_q_blk % SWA_BLOCKS_PER_STEP == 0
    return pl.pallas_call(
        functools.partial(_swa_kernel, n_lat_blk),
        grid=(b // NB_ROWS, n_q_blk // SWA_BLOCKS_PER_STEP),
        in_specs=[pl.BlockSpec(memory_space=pltpu.SMEM),
                  pl.BlockSpec((NB_ROWS, step_rows, SWA_W), lambda bi, j: (bi, j, 0)),
                  pl.BlockSpec((NB_ROWS, tt, 2 * SWA_KV_W), lambda bi, j: (bi, 0, 0))],
        out_specs=pl.BlockSpec((NB_ROWS, step_rows, SWA_W), lambda bi, j: (bi, j, 0)),
        out_shape=jax.ShapeDtypeStruct((b, n_q_blk * Q_BLOCK, SWA_W), F32),
        compiler_params=_cparams(2),
        name="swa",
    )(sinks.reshape(1, SWA_Q_HEADS), swq, swkv)


def _gdn_prep_kernel(n_lat, x_ref, hp_ref, hn_ref, ab_ref, cw_ref, alog_ref, dtb_ref, ones_ref,
                     qkv_ref, gb_ref):
    t = pl.program_id(1)
    has_prev = ((t > 0) & (t < n_lat)).astype(F32)
    has_next = (t < n_lat - 1).astype(F32)
    ones_bd = ones_ref[...]

    def l2n(u):
        return u * lax.rsqrt(_head_sum(u * u, ones_bd) + EPS)

    for bb in range(NB_WIDE):
        x = x_ref[bb]
        row = lax.broadcasted_iota(jnp.int32, x.shape, 0)
        prev_row = hp_ref[bb, 7:8, :] * has_prev
        next_row = hn_ref[bb, 0:1, :] * has_next
        xm1 = jnp.where(row == 0, prev_row, pltpu.roll(x, 1, 0))
        xp1 = jnp.where(row == TILE - 1, next_row, pltpu.roll(x, TILE - 1, 0))
        conv = xm1 * cw_ref[0:1, :] + x * cw_ref[1:2, :] + xp1 * cw_ref[2:3, :]
        s = _silu(conv)
        qkv_ref[bb, :, 0:GDN_W] = l2n(s[:, 0:GDN_W]) * (HEAD_DIM ** -0.5)
        qkv_ref[bb, :, GDN_W:2 * GDN_W] = l2n(s[:, GDN_W:2 * GDN_W])
        qkv_ref[bb, :, 2 * GDN_W:3 * GDN_W] = s[:, 2 * GDN_W:3 * GDN_W]

        ab = ab_ref[bb]
        lane = lax.broadcasted_iota(jnp.int32, ab.shape, 1)
        g = -jnp.exp(alog_ref[...]) * jax.nn.softplus(ab + dtb_ref[...])
        beta = jax.nn.sigmoid(ab)
        gb_ref[bb] = jnp.where(lane < 2 * GDN_HEADS, g, jnp.where(lane < 4 * GDN_HEADS, beta, 0.0))


def _gdn_prep_call(gqkv, ab, conv_w, a_log, dt_bias, ones_bd, n_lat):
    b, tt, w = gqkv.shape
    nt = tt // TILE
    sub = 8
    per = TILE // sub
    last = tt // sub - 1
    pad = LANES - 2 * GDN_HEADS
    alog_pad = jnp.pad(a_log.reshape(1, 2 * GDN_HEADS), ((0, 0), (0, pad)))
    dtb_pad = jnp.pad(dt_bias.reshape(1, 2 * GDN_HEADS), ((0, 0), (0, pad)))
    return pl.pallas_call(
        functools.partial(_gdn_prep_kernel, n_lat),
        grid=(b // NB_WIDE, nt),
        in_specs=[pl.BlockSpec((NB_WIDE, TILE, w), lambda bi, t: (bi, t, 0)),
                  pl.BlockSpec((NB_WIDE, sub, w), lambda bi, t: (bi, jnp.maximum(t * per - 1, 0), 0)),
                  pl.BlockSpec((NB_WIDE, sub, w), lambda bi, t: (bi, jnp.minimum((t + 1) * per, last), 0)),
                  pl.BlockSpec((NB_WIDE, TILE, LANES), lambda bi, t: (bi, t, 0)),
                  pl.BlockSpec((CONV_W, w), lambda bi, t: (0, 0)),
                  pl.BlockSpec((1, LANES), lambda bi, t: (0, 0)),
                  pl.BlockSpec((1, LANES), lambda bi, t: (0, 0)),
                  pl.BlockSpec((GDN_W, GDN_W), lambda bi, t: (0, 0))],
        out_specs=[pl.BlockSpec((NB_WIDE, TILE, w), lambda bi, t: (bi, t, 0)),
                   pl.BlockSpec((NB_WIDE, TILE, LANES), lambda bi, t: (bi, t, 0))],
        out_shape=[jax.ShapeDtypeStruct((b, tt, w), F32), jax.ShapeDtypeStruct((b, tt, LANES), F32)],
        compiler_params=_cparams(2),
        name="gdn_prep",
    )(gqkv, gqkv, gqkv, ab, conv_w, alog_pad, dtb_pad, ones_bd)


N_SUB = TILE // GDN_CHUNK


def _gdn_kernel(n_lat, ones_ref, tri_ref, xf_ref, gbf_ref, xb_ref, gbb_ref,
                of_ref, ob_ref, sf_ref, sb_ref):
    c = pl.program_id(1)

    @pl.when(c == 0)
    def _():
        sf_ref[...] = jnp.zeros_like(sf_ref)
        sb_ref[...] = jnp.zeros_like(sb_ref)

    ones_bd = ones_ref[...]
    masks = _head_masks(TILE, GDN_W)
    sub_masks = _head_masks(GDN_CHUNK, GDN_W)
    ri = lax.broadcasted_iota(jnp.int32, (TILE, TILE), 0)
    ci = lax.broadcasted_iota(jnp.int32, (TILE, TILE), 1)
    bd = (ri // GDN_CHUNK) == (ci // GDN_CHUNK)
    x_refs = (xf_ref, xb_ref)
    gb_refs = (gbf_ref, gbb_ref)
    s_refs = (sf_ref, sb_ref)
    o_refs = (of_ref, ob_ref)
    dirs = tuple(range(2 * NB))

    def expand(x, first_col):
        e = jnp.zeros((TILE, GDN_W), F32)
        for h in range(GDN_HEADS):
            e = jnp.where(masks[h], x[:, first_col + h:first_col + h + 1], e)
        return e

    pre = []
    for d in dirs:
        blk = x_refs[d % 2][d // 2]
        qn = blk[:, 0:GDN_W]
        kn = blk[:, GDN_W:2 * GDN_W]
        v = blk[:, 2 * GDN_W:3 * GDN_W]
        gbv = gb_refs[d % 2][d // 2]
        gc = _dot_sel_x(tri_ref[d % 2], gbv)
        gtot = _dot_sel_x(ones_bd, gbv)
        gc_e = expand(gc, (d % 2) * GDN_HEADS)
        gtot_e = expand(gtot, (d % 2) * GDN_HEADS)
        beta_e = expand(gbv, 2 * GDN_HEADS + (d % 2) * GDN_HEADS)
        egc = jnp.exp(gc_e)
        pre.append(dict(qn=qn, kn=kn, kn_b=kn.astype(BF16), gb_t=gbv.T, gc=gc, gc_t=gc.T,
                        vb_t=(v * beta_e).T, kb_t=(kn * beta_e * egc).T, qd=qn * egc,
                        kt=kn * jnp.exp(gtot_e - gc_e), gtot_dec=jnp.exp(gtot_e),
                        causal=bd & ((ri >= ci) if d % 2 == 0 else (ri <= ci)),
                        causal_t=bd & ((ri <= ci) if d % 2 == 0 else (ri >= ci))))

    chains = [(d, h) for h in range(GDN_HEADS) for d in dirs]
    m_t, z_acc, attn = {}, {}, {}
    for (d, h) in chains:
        p = pre[d]
        r = (d % 2) * GDN_HEADS + h
        kk = _dot_nt(jnp.where(masks[h], p["kn"], 0.0).astype(BF16), p["kn_b"])
        qk = _dot_nt(jnp.where(masks[h], p["qn"], 0.0).astype(BF16), p["kn_b"])
        diff = p["gc"][:, r:r + 1] - p["gc_t"][r:r + 1, :]
        attn[(d, h)] = qk * jnp.exp(jnp.where(p["causal"], diff, NEG_INF))
        brow = p["gb_t"][2 * GDN_HEADS + r:2 * GDN_HEADS + r + 1, :]
        m_t[(d, h)] = jnp.where(ri == ci, 0.0,
                                kk * brow * jnp.exp(jnp.where(p["causal_t"], -diff, NEG_INF)))
        z_acc[(d, h)] = jnp.concatenate([p["vb_t"][h * HEAD_DIM:(h + 1) * HEAD_DIM],
                                         p["kb_t"][h * HEAD_DIM:(h + 1) * HEAD_DIM]], axis=0)

    def off_diag(s_blk):
        return ((ri // (2 * s_blk)) == (ci // (2 * s_blk))) & ((ri // s_blk) != (ci // s_blk))

    x_inv = {}
    for ch in chains:
        x_inv[ch] = (ri == ci).astype(F32) - jnp.where(off_diag(1), m_t[ch], 0.0)
    s_blk = 2
    while s_blk < GDN_CHUNK // 2:
        for ch in chains:
            xb = x_inv[ch].astype(BF16)
            pc = _dot(xb, jnp.where(off_diag(s_blk), m_t[ch], 0.0).astype(BF16))
            x_inv[ch] = x_inv[ch] - _dot(pc.astype(BF16), xb)
        s_blk *= 2
    for ch in chains:
        xb = x_inv[ch].astype(BF16)
        y = _dot(z_acc[ch].astype(BF16), xb)
        yc = _dot(y.astype(BF16), jnp.where(off_diag(s_blk), m_t[ch], 0.0).astype(BF16))
        z_acc[ch] = y - _dot(yc.astype(BF16), xb)

    u = [jnp.concatenate([z_acc[(d, h)][0:HEAD_DIM] for h in range(GDN_HEADS)], axis=0).T for d in dirs]
    w = [jnp.concatenate([z_acc[(d, h)][HEAD_DIM:2 * HEAD_DIM] for h in range(GDN_HEADS)], axis=0).T
         for d in dirs]

    state_bd = (ri // HEAD_DIM) == (ci // HEAD_DIM)
    kt_t = [pre[d]["kt"].T for d in dirs]
    s = [s_refs[d % 2][d // 2] for d in dirs]
    outs = [[None] * N_SUB for _ in dirs]
    for i in range(N_SUB):
        for d in dirs:
            sub = i if d % 2 == 0 else N_SUB - 1 - i
            r0 = sub * GDN_CHUNK
            wq = jnp.concatenate([w[d][r0:r0 + GDN_CHUNK], pre[d]["qd"][r0:r0 + GDN_CHUNK]], axis=0)
            ws = _dot(wq.astype(BF16), s[d].astype(BF16))
            v_new = u[d][r0:r0 + GDN_CHUNK] - ws[0:GDN_CHUNK]
            o_sub = ws[GDN_CHUNK:2 * GDN_CHUNK]
            for h in range(GDN_HEADS):
                a_blk = attn[(d, h)][r0:r0 + GDN_CHUNK, r0:r0 + GDN_CHUNK].astype(BF16)
                o_sub = o_sub + _dot(a_blk, jnp.where(sub_masks[h], v_new, 0.0).astype(BF16))
            outs[d][sub] = o_sub
            upd = _dot(kt_t[d][:, r0:r0 + GDN_CHUNK].astype(BF16), v_new.astype(BF16))
            s[d] = s[d] * pre[d]["gtot_dec"][r0:r0 + 1, :] + jnp.where(state_bd, upd, 0.0)
    for d in dirs:
        s_refs[d % 2][d // 2] = s[d]
        o_refs[d % 2][d // 2] = jnp.concatenate(outs[d], axis=0)


def _gdn_consts():
    i = np.arange(TILE)
    same = (i[:, None] // GDN_CHUNK) == (i[None, :] // GDN_CHUNK)
    tri = np.stack([same & (i[None, :] <= i[:, None]), same & (i[None, :] >= i[:, None])])
    return jnp.asarray(same, BF16), jnp.asarray(tri, BF16)


def _gdn_call(qkvn, gb, consts, n_lat):
    b, tt, w = qkvn.shape
    nt = tt // TILE
    ones_chunk, tri = consts

    def f_idx(bi, c):
        return (bi, jnp.where(c == 0, n_lat, c - 1), 0)

    def b_idx(bi, c):
        return (bi, jnp.where(c == 0, n_lat, n_lat - c), 0)

    return pl.pallas_call(
        functools.partial(_gdn_kernel, n_lat),
        grid=(b // NB, nt),
        in_specs=[pl.BlockSpec((TILE, TILE), lambda bi, c: (0, 0)),
                  pl.BlockSpec((2, TILE, TILE), lambda bi, c: (0, 0, 0)),
                  pl.BlockSpec((NB, TILE, w), f_idx),
                  pl.BlockSpec((NB, TILE, LANES), f_idx),
                  pl.BlockSpec((NB, TILE, w), b_idx),
                  pl.BlockSpec((NB, TILE, LANES), b_idx)],
        out_specs=[pl.BlockSpec((NB, TILE, GDN_W), f_idx), pl.BlockSpec((NB, TILE, GDN_W), b_idx)],
        out_shape=[jax.ShapeDtypeStruct((b, tt, GDN_W), F32)] * 2,
        scratch_shapes=[pltpu.VMEM((NB, GDN_W, GDN_W), F32), pltpu.VMEM((NB, GDN_W, GDN_W), F32)],
        compiler_params=_cparams(2),
        name="gdn",
    )(ones_chunk, tri, qkvn, gb, qkvn, gb)


FF_CHUNK = 512


def _post_kernel(d_ff, final, n_lat_tiles, with_ctx, x_ref, *refs):
    if with_ctx:
        c_ref, refs = refs[0], refs[1:]
        is_lat = pl.program_id(1) < n_lat_tiles
        xs = [jnp.where(is_lat, x_ref[bb], c_ref[bb]) for bb in range(NB)]
    else:
        xs = [x_ref[bb] for bb in range(NB)]
    (rf_ref, rb_ref, rg_ref, ys_ref, gf_ref, gb_ref, gz_ref, g1_ref, sh_ref, sc_ref, g2_ref, nw_ref,
     fw_ref, rnw_ref, gnw_ref, ones_ref, wo_ref, w1_ref, w2_ref, o_ref) = refs
    ones_head = ones_ref[...]

    def rows(per_sample):
        return jnp.concatenate([per_sample(bb) for bb in range(NB)], axis=0)

    def head_out(f_ref, b_ref, gate_ref, w_ref):
        def one(bb):
            o = f_ref[bb] + b_ref[bb]
            ms = _head_sum(o * o, ones_head) * (1.0 / HEAD_DIM)
            return (o * lax.rsqrt(ms + EPS) * w_ref[...] * _silu(gate_ref[bb])).astype(BF16)
        return rows(one)

    mix = (_dot(head_out(rf_ref, rb_ref, rg_ref, rnw_ref), wo_ref[0:RET_W, :])
           + _dot(rows(lambda bb: ys_ref[bb].astype(BF16)), wo_ref[RET_W:RET_W + SWA_W, :])
           + _dot(head_out(gf_ref, gb_ref, gz_ref, gnw_ref), wo_ref[RET_W + SWA_W:, :]))
    x1 = [xs[bb] + g1_ref[bb] * mix[bb * TILE:(bb + 1) * TILE] for bb in range(NB)]

    def ffn_in(bb):
        y = x1[bb] * lax.rsqrt(jnp.mean(x1[bb] * x1[bb], axis=-1, keepdims=True) + EPS) * nw_ref[...]
        return (y * (1.0 + sc_ref[bb]) + sh_ref[bb]).astype(BF16)

    h = rows(ffn_in)
    acc = jnp.zeros((NB * TILE, x1[0].shape[1]), F32)
    for c0 in range(0, d_ff, FF_CHUNK):
        cw = min(FF_CHUNK, d_ff - c0)
        gate = _dot(h, w1_ref[:, c0:c0 + cw])
        up = _dot(h, w1_ref[:, d_ff + c0:d_ff + c0 + cw])
        acc = acc + _dot((_silu(gate) * up).astype(BF16), w2_ref[c0:c0 + cw, :])
    for bb in range(NB):
        x2 = x1[bb] + g2_ref[bb] * acc[bb * TILE:(bb + 1) * TILE]
        if final:
            x2 = x2 * lax.rsqrt(jnp.mean(x2 * x2, axis=-1, keepdims=True) + EPS) * fw_ref[...]
        o_ref[bb] = x2


def _post_call(x_lat, x_ctx, ctx_tile, o_rf, o_rb, ret, y_s, o_gf, o_gb, gz, mod, norm_w, final_w,
               ret_norm_w, gdn_norm_w, ones_head, wo_bf, w1_bf, w2_bf, n_lat_tiles, ctx_row, final):
    b, _, d = x_lat.shape
    d_ff = w2_bf.shape[0]
    with_ctx = x_ctx is not None
    n_steps = n_lat_tiles + 1 if with_ctx else n_lat_tiles

    def mod_spec(k):
        return _mod_spec(d, k, n_lat_tiles if with_ctx else None, ctx_row, NB)

    def tile_spec(w):
        return pl.BlockSpec((NB, TILE, w), lambda bi, t: (bi, t, 0))

    full = lambda shape: pl.BlockSpec(shape, lambda bi, t: (0,) * len(shape))
    x_specs = _row_source_specs(d, n_lat_tiles, ctx_tile, NB) if with_ctx else [tile_spec(d)]
    x_args = (x_lat, x_ctx) if with_ctx else (x_lat,)
    return pl.pallas_call(
        functools.partial(_post_kernel, d_ff, final, n_lat_tiles, with_ctx),
        grid=(b // NB, n_steps),
        in_specs=x_specs + [
                  tile_spec(RET_W), tile_spec(RET_W),
                  pl.BlockSpec((NB, TILE, RET_W), lambda bi, t: (bi, t, 3)),
                  tile_spec(SWA_W), tile_spec(GDN_W), tile_spec(GDN_W), tile_spec(GDN_W),
                  mod_spec(2), mod_spec(3), mod_spec(4), mod_spec(5),
                  full((1, d)), full((1, d)), full((1, RET_W)), full((1, GDN_W)), full(ones_head.shape),
                  full(wo_bf.shape), full(w1_bf.shape), full(w2_bf.shape)],
        out_specs=tile_spec(d),
        out_shape=jax.ShapeDtypeStruct((b, n_steps * TILE, d), F32),
        compiler_params=_cparams(2),
        name="out_proj_ffn",
    )(*x_args, o_rf, o_rb, ret, y_s, o_gf, o_gb, gz, mod, mod, mod, mod, norm_w.reshape(1, d),
      final_w.reshape(1, d), ret_norm_w.reshape(1, RET_W),
      jnp.tile(gdn_norm_w, GDN_HEADS).reshape(1, GDN_W), ones_head, wo_bf, w1_bf, w2_bf)


def _rope_tables(n_lat_rows, n_ctx_rows):
    n_grid_rows = n_lat_rows // GRID_W
    freqs = jnp.asarray(ROPE_BASE, F32) ** (-jnp.arange(N_FREQ, dtype=F32) / N_FREQ)
    ang_r = jnp.arange(n_grid_rows, dtype=F32)[:, None] * freqs[None, :]
    ang_c = jnp.arange(GRID_W, dtype=F32)[:, None] * freqs[None, :]

    def table(fr, fc, sign):
        r = jnp.broadcast_to(fr[:, None, :], (n_grid_rows, GRID_W, N_FREQ))
        c = jnp.broadcast_to(fc[None, :, :], (n_grid_rows, GRID_W, N_FREQ))
        head = jnp.concatenate([sign * r, r, sign * c, c], axis=-1).reshape(n_lat_rows, HEAD_DIM)
        return jnp.tile(head, (1, LANES // HEAD_DIM))

    cos = table(jnp.cos(ang_r), jnp.cos(ang_c), 1.0)
    sin = table(jnp.sin(ang_r), jnp.sin(ang_c), -1.0)
    cos = jnp.concatenate([cos, jnp.ones((n_ctx_rows, LANES), F32)], axis=0)
    sin = jnp.concatenate([sin, jnp.zeros((n_ctx_rows, LANES), F32)], axis=0)
    return cos, sin


def kernel(x, c, ctx, c_ctx, ada_w, ada_b, norm_mix_w, norm_ffn_w, w_in, ret_rate, ret_norm_w,
           swa_sinks, gdn_conv_w, gdn_a_log, gdn_dt_bias, gdn_norm_w, w_out, w_ffn_in, w_ffn_out,
           final_norm_w):
    b, l, d = x.shape
    lc = ctx.shape[1]
    depth = ada_w.shape[0]
    assert l % TILE == 0 and lc == TILE and d % LANES == 0
    assert b % NB == 0 and b % NB_ROWS == 0 and b % NB_WIDE == 0 and NB_ROWS % NB == 0 and b + NB_ROWS <= MOD_ROWS
    n_lat = l // TILE

    x_lat, x_ctx, ctx_tile = x, ctx, 0
    c_pad = jnp.zeros((MOD_ROWS, d), F32).at[:b].set(c).at[b:b + NB_ROWS].set(c_ctx)
    mod = _ada_call(c_pad, ada_w, ada_b)
    cos_t, sin_t = _rope_tables(l, lc)
    consts = _gdn_consts()
    i = np.arange(RET_W)
    ones_head = jnp.asarray((i[:, None] // HEAD_DIM) == (i[None, :] // HEAD_DIM), BF16)
    log_gamma = jnp.log1p(-jnp.exp2(-ret_rate.astype(F32)))

    for layer in range(depth):
        last = layer == depth - 1
        mod_l = mod[layer].reshape(MOD_ROWS, 1, N_MOD * d)
        w_in_bf = jnp.pad(w_in[layer], ((0, 0), (0, IN_W_PAD - IN_W))).astype(BF16)
        ret, swq, swkv, gqkv, gz, ab = _inproj_call(x_lat, x_ctx, ctx_tile, mod_l, norm_mix_w[layer],
                                                    w_in_bf, cos_t, sin_t, n_lat, b)
        o_rf, o_rb = _ret_call(ret, log_gamma[layer], n_lat)
        n_lat_blk = l // Q_BLOCK
        n_q_blk = n_lat_blk if last else n_lat_blk + lc // Q_BLOCK
        y_s = _swa_call(swq, swkv, swa_sinks[layer], n_lat_blk, n_q_blk)
        qkvn, gb = _gdn_prep_call(gqkv, ab, gdn_conv_w[layer], gdn_a_log[layer], gdn_dt_bias[layer],
                                  ones_head, n_lat)
        o_gf, o_gb = _gdn_call(qkvn, gb, consts, n_lat)
        x_new = _post_call(x_lat, None if last else x_ctx, ctx_tile, o_rf, o_rb, ret, y_s, o_gf, o_gb, gz,
                           mod_l, norm_ffn_w[layer], final_norm_w, ret_norm_w[layer], gdn_norm_w[layer],
                           ones_head, w_out[layer].astype(BF16), w_ffn_in[layer].astype(BF16),
                           w_ffn_out[layer].astype(BF16), n_lat, b, last)
        x_lat, x_ctx, ctx_tile = x_new, x_new, n_lat
    return x_lat
```

```python
import functools

import numpy as np
import jax
import jax.numpy as jnp
from jax import lax
from jax.experimental import pallas as pl
from jax.experimental.pallas import tpu as pltpu

F32 = jnp.float32
BF16 = jnp.bfloat16

HEAD_DIM = 64
N_FREQ = HEAD_DIM // 4
GRID_W = 64
RET_HEADS = 4
SWA_Q_HEADS = 8
SWA_KV_HEADS = 2
GDN_HEADS = 4
RET_W = RET_HEADS * HEAD_DIM
SWA_W = SWA_Q_HEADS * HEAD_DIM
SWA_KV_W = SWA_KV_HEADS * HEAD_DIM
GDN_W = GDN_HEADS * HEAD_DIM
WINDOW = 128
Q_BLOCK = 128
GDN_CHUNK = 64
CONV_W = 3
ROPE_BASE = 10000.0
EPS = 1e-6
NEG_INF = -1e30
N_MOD = 6

LANES = 128
TILE = 256
MOD_ROWS = 16
NB = 2
NB_ROWS = 4
NB_WIDE = 8
VMEM_LIMIT = 56 * 1024 * 1024

COL_RET = 0
COL_SWA_Q = 4 * RET_W
COL_SWA_KV = COL_SWA_Q + SWA_W
COL_GDN_QKV = COL_SWA_KV + 2 * SWA_KV_W
COL_GDN_Z = COL_GDN_QKV + 3 * GDN_W
COL_AB = COL_GDN_Z + GDN_W
IN_W = COL_AB + 4 * GDN_HEADS
IN_W_PAD = COL_AB + LANES


def _dot(a, b):
    return jnp.dot(a, b, preferred_element_type=F32)


def _dot_nt(a, b):
    return lax.dot_general(a, b, (((1,), (1,)), ((), ())), preferred_element_type=F32)


def _split3(x):
    hi = x.astype(BF16)
    r = x - hi.astype(F32)
    mid = r.astype(BF16)
    lo = (r - mid.astype(F32)).astype(BF16)
    return hi, mid, lo


def _dot_x_sel(x, sel):
    hi, mid, lo = _split3(x)
    return _dot(hi, sel) + _dot(mid, sel) + _dot(lo, sel)


def _dot_sel_x(sel, x):
    hi, mid, lo = _split3(x)
    return _dot(sel, hi) + _dot(sel, mid) + _dot(sel, lo)


def _silu(x):
    return x * jax.nn.sigmoid(x)


def _head_masks(rows, width):
    lane = lax.broadcasted_iota(jnp.int32, (rows, width), 1)
    return [(lane // HEAD_DIM) == h for h in range(width // HEAD_DIM)]


def _head_sum(x, ones_bd):
    return _dot_x_sel(x, ones_bd)


def _cparams(n_axes):
    return pltpu.CompilerParams(dimension_semantics=("arbitrary",) * n_axes,
                                vmem_limit_bytes=VMEM_LIMIT)


def _ada_kernel(c_ref, w_ref, b_ref, o_ref):
    cs = _silu(c_ref[...])
    o_ref[0] = _dot(cs.astype(BF16), w_ref[0].astype(BF16)) + b_ref[0]


def _ada_call(c_pad, ada_w, ada_b):
    depth, d, n = ada_w.shape
    tn = 1536
    return pl.pallas_call(
        _ada_kernel,
        grid=(depth, n // tn),
        in_specs=[pl.BlockSpec((MOD_ROWS, d), lambda l, j: (0, 0)),
                  pl.BlockSpec((1, d, tn), lambda l, j: (l, 0, j)),
                  pl.BlockSpec((1, 1, tn), lambda l, j: (l, 0, j))],
        out_specs=pl.BlockSpec((1, MOD_ROWS, tn), lambda l, j: (l, 0, j)),
        out_shape=jax.ShapeDtypeStruct((depth, MOD_ROWS, n), F32),
        compiler_params=_cparams(2),
        name="ada_mod",
    )(c_pad, ada_w, ada_b.reshape(depth, 1, n))


def _rope(t, cos, sin_signed):
    lane = lax.broadcasted_iota(jnp.int32, t.shape, 1)
    first_half = (lane % (2 * N_FREQ)) < N_FREQ
    partner = jnp.where(first_half, pltpu.roll(t, LANES - N_FREQ, 1), pltpu.roll(t, N_FREQ, 1))
    return t * cos + partner * sin_signed


def _inproj_kernel(n_lat_tiles, x_ref, c_ref, sh_ref, sc_ref, nw_ref, w_ref, cos_ref, sin_ref,
                   ret_ref, swq_ref, swkv_ref, gqkv_ref, gz_ref, ab_ref):
    is_lat = pl.program_id(1) < n_lat_tiles
    hs = []
    for bb in range(NB_ROWS):
        x = jnp.where(is_lat, x_ref[bb], c_ref[bb])
        y = x * lax.rsqrt(jnp.mean(x * x, axis=-1, keepdims=True) + EPS) * nw_ref[...]
        hs.append((y * (1.0 + sc_ref[bb]) + sh_ref[bb]).astype(BF16))
    h = jnp.concatenate(hs, axis=0)
    cos = cos_ref[...]
    sin = sin_ref[...]

    def proj(col, width, out_ref, out_col, roped_lanes):
        r = _dot(h, w_ref[:, col:col + width])
        for bb in range(NB_ROWS):
            for g in range(width // LANES):
                rg = r[bb * TILE:(bb + 1) * TILE, g * LANES:(g + 1) * LANES]
                if g * LANES < roped_lanes:
                    rg = _rope(rg, cos, sin)
                out_ref[bb, :, out_col + g * LANES:out_col + (g + 1) * LANES] = rg

    for g in range(4):
        proj(COL_RET + g * RET_W, RET_W, ret_ref, g * RET_W, RET_W if g < 2 else 0)
    for g in range(SWA_W // TILE):
        proj(COL_SWA_Q + g * TILE, TILE, swq_ref, g * TILE, TILE)
    proj(COL_SWA_KV, 2 * SWA_KV_W, swkv_ref, 0, SWA_KV_W)
    for g in range(3):
        proj(COL_GDN_QKV + g * GDN_W, GDN_W, gqkv_ref, g * GDN_W, 0)
    proj(COL_GDN_Z, GDN_W, gz_ref, 0, 0)
    proj(COL_AB, LANES, ab_ref, 0, 0)


def _row_source_specs(d, n_lat_tiles, ctx_tile, nb):
    return [pl.BlockSpec((nb, TILE, d), lambda bi, t: (bi, jnp.minimum(t, n_lat_tiles - 1), 0)),
            pl.BlockSpec((nb, TILE, d), lambda bi, t: (bi, ctx_tile, 0))]


def _mod_spec(d, k, n_lat_tiles, ctx_row, nb):
    if n_lat_tiles is None:
        return pl.BlockSpec((nb, 1, d), lambda bi, t: (bi, 0, k))
    return pl.BlockSpec((nb, 1, d), lambda bi, t: (jnp.where(t >= n_lat_tiles, ctx_row // nb, bi), 0, k))


def _layer_weight_spec(w, layer):
    return pl.BlockSpec((None,) + w.shape[1:], lambda bi, t: (layer, 0, 0), pipeline_mode=pl.Buffered(1))


def _inproj_call(x_lat, x_ctx, ctx_tile, mod, norm_w, w_in_bf, layer, cos_t, sin_t, n_lat_tiles, ctx_row):
    b, _, d = x_lat.shape
    nt = n_lat_tiles + 1
    tt = nt * TILE
    widths = (4 * RET_W, SWA_W, 2 * SWA_KV_W, 3 * GDN_W, GDN_W, LANES)
    return pl.pallas_call(
        functools.partial(_inproj_kernel, n_lat_tiles),
        grid=(b // NB_ROWS, nt),
        in_specs=_row_source_specs(d, n_lat_tiles, ctx_tile, NB_ROWS) + [
                  _mod_spec(d, 0, n_lat_tiles, ctx_row, NB_ROWS),
                  _mod_spec(d, 1, n_lat_tiles, ctx_row, NB_ROWS),
                  pl.BlockSpec((1, d), lambda bi, t: (0, 0)),
                  _layer_weight_spec(w_in_bf, layer),
                  pl.BlockSpec((TILE, LANES), lambda bi, t: (t, 0)),
                  pl.BlockSpec((TILE, LANES), lambda bi, t: (t, 0))],
        out_specs=[pl.BlockSpec((NB_ROWS, TILE, w), lambda bi, t: (bi, t, 0)) for w in widths],
        out_shape=[jax.ShapeDtypeStruct((b, tt, w), F32) for w in widths],
        compiler_params=_cparams(2),
        name="in_proj",
    )(x_lat, x_ctx, mod, mod, norm_w.reshape(1, d), w_in_bf, cos_t, sin_t)


def _ret_kernel(lg_ref, lge_ref, rf_ref, rb_ref, of_ref, ob_ref, sf_ref, sb_ref):
    c = pl.program_id(1)

    @pl.when(c == 0)
    def _():
        sf_ref[...] = jnp.zeros_like(sf_ref)
        sb_ref[...] = jnp.zeros_like(sb_ref)

    masks = _head_masks(TILE, RET_W)
    ri = lax.broadcasted_iota(jnp.int32, (TILE, TILE), 0)
    ci = lax.broadcasted_iota(jnp.int32, (TILE, TILE), 1)
    bd = (ri // HEAD_DIM) == (ci // HEAD_DIM)
    idx = ri.astype(F32)
    diff = (ri - ci).astype(F32)
    dm = [jnp.where(diff >= 0, jnp.exp(lg_ref[0, h] * jnp.maximum(diff, 0.0)), 0.0)
          + jnp.where(diff <= 0, jnp.exp(lg_ref[1, h] * jnp.maximum(-diff, 0.0)), 0.0)
          for h in range(RET_HEADS)]
    dec = []
    for direction in (0, 1):
        lge = lge_ref[direction:direction + 1, :]
        if direction == 0:
            dec.append((jnp.exp(lge * (idx + 1.0)), jnp.exp(lge * (TILE - 1.0 - idx)),
                        jnp.exp(lge * float(TILE))))
        else:
            dec.append((jnp.exp(lge * (TILE - idx)), jnp.exp(lge * idx), jnp.exp(lge * float(TILE))))

    def sweep(direction, bb, r_ref, s_ref):
        blk = r_ref[bb]
        q = blk[:, 0:RET_W] * (HEAD_DIM ** -0.5)
        k = blk[:, RET_W:2 * RET_W]
        v = blk[:, 2 * RET_W:3 * RET_W]
        q_dec, k_dec, c_dec = dec[direction]
        s = s_ref[bb]
        cross = _dot((q * q_dec).astype(BF16), s.astype(BF16))
        kt = (k * k_dec).T
        upd = _dot(kt.astype(BF16), v.astype(BF16))
        s_ref[bb] = s * c_dec + jnp.where(bd, upd, 0.0)
        return cross, q, k, v

    for bb in range(NB_WIDE):
        o, q, k, v = sweep(0, bb, rf_ref, sf_ref)
        kb = k.astype(BF16)
        for h in range(RET_HEADS):
            sc = _dot_nt(jnp.where(masks[h], q, 0.0).astype(BF16), kb)
            o = o + _dot((sc * dm[h]).astype(BF16), jnp.where(masks[h], v, 0.0).astype(BF16))
        of_ref[bb] = o
        ob_ref[bb] = sweep(1, bb, rb_ref, sb_ref)[0]


def _ret_call(ret, log_gamma, n_lat):
    b, tt, _ = ret.shape
    nt = tt // TILE
    lge = jnp.repeat(log_gamma, HEAD_DIM, axis=-1)

    def f_idx(bi, c):
        return (bi, jnp.where(c == 0, n_lat, c - 1), 0)

    def b_idx(bi, c):
        return (bi, jnp.where(c == 0, n_lat, n_lat - c), 0)

    return pl.pallas_call(
        _ret_kernel,
        grid=(b // NB_WIDE, nt),
        in_specs=[pl.BlockSpec(memory_space=pltpu.SMEM),
                  pl.BlockSpec((2, RET_W), lambda bi, c: (0, 0)),
                  pl.BlockSpec((NB_WIDE, TILE, 3 * RET_W), f_idx),
                  pl.BlockSpec((NB_WIDE, TILE, 3 * RET_W), b_idx)],
        out_specs=[pl.BlockSpec((NB_WIDE, TILE, RET_W), f_idx), pl.BlockSpec((NB_WIDE, TILE, RET_W), b_idx)],
        out_shape=[jax.ShapeDtypeStruct((b, tt, RET_W), F32)] * 2,
        scratch_shapes=[pltpu.VMEM((NB_WIDE, RET_W, RET_W), F32), pltpu.VMEM((NB_WIDE, RET_W, RET_W), F32)],
        compiler_params=_cparams(2),
        name="retention",
    )(log_gamma, lge, ret, ret)


SWA_BLOCKS_PER_STEP = 2


def _swa_kernel(n_lat_blk, sink_ref, q_ref, kv_ref, y_ref):
    for bb in range(NB_ROWS):
        for i in range(SWA_BLOCKS_PER_STEP):
            _swa_block(n_lat_blk, sink_ref, q_ref.at[bb:bb + 1], kv_ref.at[bb:bb + 1], y_ref.at[bb:bb + 1],
                       pl.program_id(1) * SWA_BLOCKS_PER_STEP + i, slice(i * Q_BLOCK, (i + 1) * Q_BLOCK))


def _swa_block(n_lat_blk, sink_ref, q_ref, kv_ref, y_ref, j, q_rows_in_step):
    is_lat = j < n_lat_blk
    jc = jnp.minimum(j, n_lat_blk - 1)
    jp = jnp.maximum(jc - 1, 0)
    jn = jnp.minimum(jc + 1, n_lat_blk - 1)
    lat_rows = n_lat_blk * Q_BLOCK

    def piece(blk):
        return kv_ref[0, pl.ds(pl.multiple_of(blk * Q_BLOCK, Q_BLOCK), Q_BLOCK), :]

    kv_all = jnp.concatenate([piece(jp), piece(jc), piece(jn), kv_ref[0, lat_rows:, :]], axis=0)
    k_all = kv_all[:, 0:SWA_KV_W].astype(BF16)
    v_all = kv_all[:, SWA_KV_W:2 * SWA_KV_W].astype(BF16)
    n_piece = k_all.shape[0] // Q_BLOCK

    group = SWA_Q_HEADS // SWA_KV_HEADS
    rows = group * Q_BLOCK
    ql = lax.broadcasted_iota(jnp.int32, (rows, Q_BLOCK), 0) % Q_BLOCK
    kl = lax.broadcasted_iota(jnp.int32, (rows, Q_BLOCK), 1)
    ok_prev = (kl >= ql) & is_lat & (j >= 1)
    ok_next = (kl <= ql) & is_lat & (j <= n_lat_blk - 2)
    lane = lax.broadcasted_iota(jnp.int32, (Q_BLOCK, LANES), 1)
    low = lane < HEAD_DIM
    out_tiles = [[None, None] for _ in range(SWA_W // LANES)]

    for g in range(SWA_KV_HEADS):
        q_rows, sink_rows = [], []
        for hh in range(group):
            head = g * group + hh
            t, e = head // 2, head % 2
            qt = q_ref[0, q_rows_in_step, t * LANES:(t + 1) * LANES] * (HEAD_DIM ** -0.5)
            qm = jnp.where(low if e == 0 else ~low, qt, 0.0)
            if e != g:
                qm = pltpu.roll(qm, HEAD_DIM, 1)
            q_rows.append(qm)
            sink_rows.append(jnp.full((Q_BLOCK, 1), sink_ref[0, head], F32))
        qs = jnp.concatenate(q_rows, axis=0).astype(BF16)
        sink = jnp.concatenate(sink_rows, axis=0)
        s = _dot_nt(qs, k_all)

        def masked(i):
            s_i = s[:, i * Q_BLOCK:(i + 1) * Q_BLOCK]
            if i == 0:
                return jnp.where(ok_prev, s_i, NEG_INF)
            if i == 1:
                return jnp.where(is_lat, s_i, NEG_INF)
            if i == 2:
                return jnp.where(ok_next, s_i, NEG_INF)
            return s_i

        mx = masked(0)
        for i in range(1, n_piece):
            mx = jnp.maximum(mx, masked(i))
        m = jnp.maximum(jnp.max(mx, axis=-1, keepdims=True), sink)
        pr = [jnp.exp(masked(i) - m) for i in range(n_piece)]
        acc = pr[0]
        for p_i in pr[1:]:
            acc = acc + p_i
        den = jnp.sum(acc, axis=-1, keepdims=True) + jnp.exp(sink - m)
        pb = jnp.concatenate([p_i.astype(BF16) for p_i in pr], axis=1)
        o = _dot(pb, v_all) / den
        for hh in range(group):
            head = g * group + hh
            t, e = head // 2, head % 2
            oh = o[hh * Q_BLOCK:(hh + 1) * Q_BLOCK]
            if e != g:
                oh = pltpu.roll(oh, HEAD_DIM, 1)
            out_tiles[t][e] = oh
    for t in range(SWA_W // LANES):
        y_ref[0, q_rows_in_step, t * LANES:(t + 1) * LANES] = jnp.where(low, out_tiles[t][0], out_tiles[t][1])


def _swa_call(swq, swkv, sinks, n_lat_blk, n_q_blk):
    b, tt, _ = swq.shape
    step_rows = SWA_BLOCKS_PER_STEP * Q_BLOCK
    assert n_q_blk % SWA_BLOCKS_PER_STEP == 0
    return pl.pallas_call(
        functools.partial(_swa_kernel, n_lat_blk),
        grid=(b // NB_ROWS, n_q_blk // SWA_BLOCKS_PER_STEP),
        in_specs=[pl.BlockSpec(memory_space=pltpu.SMEM),
                  pl.BlockSpec((NB_ROWS, step_rows, SWA_W), lambda bi, j: (bi, j, 0)),
                  pl.BlockSpec((NB_ROWS, tt, 2 * SWA_KV_W), lambda bi, j: (bi, 0, 0))],
        out_specs=pl.BlockSpec((NB_ROWS, step_rows, SWA_W), lambda bi, j: (bi, j, 0)),
        out_shape=jax.ShapeDtypeStruct((b, n_q_blk * Q_BLOCK, SWA_W), F32),
        compiler_params=_cparams(2),
        name="swa",
    )(sinks.reshape(1, SWA_Q_HEADS), swq, swkv)


def _gdn_prep_kernel(n_lat, x_ref, hp_ref, hn_ref, ab_ref, cw_ref, alog_ref, dtb_ref, ones_ref,
                     qkv_ref, gb_ref):
    t = pl.program_id(1)
    has_prev = ((t > 0) & (t < n_lat)).astype(F32)
    has_next = (t < n_lat - 1).astype(F32)
    ones_bd = ones_ref[...]

    def l2n(u):
        return u * lax.rsqrt(_head_sum(u * u, ones_bd) + EPS)

    for bb in range(NB_WIDE):
        x = x_ref[bb]
        row = lax.broadcasted_iota(jnp.int32, x.shape, 0)
        prev_row = hp_ref[bb, 7:8, :] * has_prev
        next_row = hn_ref[bb, 0:1, :] * has_next
        xm1 = jnp.where(row == 0, prev_row, pltpu.roll(x, 1, 0))
        xp1 = jnp.where(row == TILE - 1, next_row, pltpu.roll(x, TILE - 1, 0))
        conv = xm1 * cw_ref[0:1, :] + x * cw_ref[1:2, :] + xp1 * cw_ref[2:3, :]
        s = _silu(conv)
        qkv_ref[bb, :, 0:GDN_W] = l2n(s[:, 0:GDN_W]) * (HEAD_DIM ** -0.5)
        qkv_ref[bb, :, GDN_W:2 * GDN_W] = l2n(s[:, GDN_W:2 * GDN_W])
        qkv_ref[bb, :, 2 * GDN_W:3 * GDN_W] = s[:, 2 * GDN_W:3 * GDN_W]

        ab = ab_ref[bb]
        lane = lax.broadcasted_iota(jnp.int32, ab.shape, 1)
        g = -jnp.exp(alog_ref[...]) * jax.nn.softplus(ab + dtb_ref[...])
        beta = jax.nn.sigmoid(ab)
        gb_ref[bb] = jnp.where(lane < 2 * GDN_HEADS, g, jnp.where(lane < 4 * GDN_HEADS, beta, 0.0))


def _gdn_prep_call(gqkv, ab, conv_w, a_log, dt_bias, ones_bd, n_lat):
    b, tt, w = gqkv.shape
    nt = tt // TILE
    sub = 8
    per = TILE // sub
    last = tt // sub - 1
    pad = LANES - 2 * GDN_HEADS
    alog_pad = jnp.pad(a_log.reshape(1, 2 * GDN_HEADS), ((0, 0), (0, pad)))
    dtb_pad = jnp.pad(dt_bias.reshape(1, 2 * GDN_HEADS), ((0, 0), (0, pad)))
    return pl.pallas_call(
        functools.partial(_gdn_prep_kernel, n_lat),
        grid=(b // NB_WIDE, nt),
        in_specs=[pl.BlockSpec((NB_WIDE, TILE, w), lambda bi, t: (bi, t, 0)),
                  pl.BlockSpec((NB_WIDE, sub, w), lambda bi, t: (bi, jnp.maximum(t * per - 1, 0), 0)),
                  pl.BlockSpec((NB_WIDE, sub, w), lambda bi, t: (bi, jnp.minimum((t + 1) * per, last), 0)),
                  pl.BlockSpec((NB_WIDE, TILE, LANES), lambda bi, t: (bi, t, 0)),
                  pl.BlockSpec((CONV_W, w), lambda bi, t: (0, 0)),
                  pl.BlockSpec((1, LANES), lambda bi, t: (0, 0)),
                  pl.BlockSpec((1, LANES), lambda bi, t: (0, 0)),
                  pl.BlockSpec((GDN_W, GDN_W), lambda bi, t: (0, 0))],
        out_specs=[pl.BlockSpec((NB_WIDE, TILE, w), lambda bi, t: (bi, t, 0)),
                   pl.BlockSpec((NB_WIDE, TILE, LANES), lambda bi, t: (bi, t, 0))],
        out_shape=[jax.ShapeDtypeStruct((b, tt, w), F32), jax.ShapeDtypeStruct((b, tt, LANES), F32)],
        compiler_params=_cparams(2),
        name="gdn_prep",
    )(gqkv, gqkv, gqkv, ab, conv_w, alog_pad, dtb_pad, ones_bd)


N_SUB = TILE // GDN_CHUNK


def _gdn_kernel(n_lat, ones_ref, tri_ref, xf_ref, gbf_ref, xb_ref, gbb_ref,
                of_ref, ob_ref, sf_ref, sb_ref):
    c = pl.program_id(1)

    @pl.when(c == 0)
    def _():
        sf_ref[...] = jnp.zeros_like(sf_ref)
        sb_ref[...] = jnp.zeros_like(sb_ref)

    ones_bd = ones_ref[...]
    masks = _head_masks(TILE, GDN_W)
    sub_masks = _head_masks(GDN_CHUNK, GDN_W)
    ri = lax.broadcasted_iota(jnp.int32, (TILE, TILE), 0)
    ci = lax.broadcasted_iota(jnp.int32, (TILE, TILE), 1)
    bd = (ri // GDN_CHUNK) == (ci // GDN_CHUNK)
    x_refs = (xf_ref, xb_ref)
    gb_refs = (gbf_ref, gbb_ref)
    s_refs = (sf_ref, sb_ref)
    o_refs = (of_ref, ob_ref)
    dirs = tuple(range(2 * NB))

    def expand(x, first_col):
        e = jnp.zeros((TILE, GDN_W), F32)
        for h in range(GDN_HEADS):
            e = jnp.where(masks[h], x[:, first_col + h:first_col + h + 1], e)
        return e

    pre = []
    for d in dirs:
        blk = x_refs[d % 2][d // 2]
        qn = blk[:, 0:GDN_W]
        kn = blk[:, GDN_W:2 * GDN_W]
        v = blk[:, 2 * GDN_W:3 * GDN_W]
        gbv = gb_refs[d % 2][d // 2]
        gc = _dot_sel_x(tri_ref[d % 2], gbv)
        gtot = _dot_sel_x(ones_bd, gbv)
        gc_e = expand(gc, (d % 2) * GDN_HEADS)
        gtot_e = expand(gtot, (d % 2) * GDN_HEADS)
        beta_e = expand(gbv, 2 * GDN_HEADS + (d % 2) * GDN_HEADS)
        egc = jnp.exp(gc_e)
        pre.append(dict(qn=qn, kn=kn, kn_b=kn.astype(BF16), gb_t=gbv.T, gc=gc, gc_t=gc.T,
                        vb_t=(v * beta_e).T, kb_t=(kn * beta_e * egc).T, qd=qn * egc,
                        kt=kn * jnp.exp(gtot_e - gc_e), gtot_dec=jnp.exp(gtot_e),
                        causal=bd & ((ri >= ci) if d % 2 == 0 else (ri <= ci)),
                        causal_t=bd & ((ri <= ci) if d % 2 == 0 else (ri >= ci))))

    chains = [(d, h) for h in range(GDN_HEADS) for d in dirs]
    m_t, z_acc, attn = {}, {}, {}
    for (d, h) in chains:
        p = pre[d]
        r = (d % 2) * GDN_HEADS + h
        kk = _dot_nt(jnp.where(masks[h], p["kn"], 0.0).astype(BF16), p["kn_b"])
        qk = _dot_nt(jnp.where(masks[h], p["qn"], 0.0).astype(BF16), p["kn_b"])
        diff = p["gc"][:, r:r + 1] - p["gc_t"][r:r + 1, :]
        attn[(d, h)] = qk * jnp.exp(jnp.where(p["causal"], diff, NEG_INF))
        brow = p["gb_t"][2 * GDN_HEADS + r:2 * GDN_HEADS + r + 1, :]
        m_t[(d, h)] = jnp.where(ri == ci, 0.0,
                                kk * brow * jnp.exp(jnp.where(p["causal_t"], -diff, NEG_INF)))
        z_acc[(d, h)] = jnp.concatenate([p["vb_t"][h * HEAD_DIM:(h + 1) * HEAD_DIM],
                                         p["kb_t"][h * HEAD_DIM:(h + 1) * HEAD_DIM]], axis=0)

    def off_diag(s_blk):
        return ((ri // (2 * s_blk)) == (ci // (2 * s_blk))) & ((ri // s_blk) != (ci // s_blk))

    x_inv = {}
    for ch in chains:
        x_inv[ch] = (ri == ci).astype(F32) - jnp.where(off_diag(1), m_t[ch], 0.0)
    s_blk = 2
    while s_blk < GDN_CHUNK // 2:
        for ch in chains:
            xb = x_inv[ch].astype(BF16)
            pc = _dot(xb, jnp.where(off_diag(s_blk), m_t[ch], 0.0).astype(BF16))
            x_inv[ch] = x_inv[ch] - _dot(pc.astype(BF16), xb)
        s_blk *= 2
    for ch in chains:
        xb = x_inv[ch].astype(BF16)
        y = _dot(z_acc[ch].astype(BF16), xb)
        yc = _dot(y.astype(BF16), jnp.where(off_diag(s_blk), m_t[ch], 0.0).astype(BF16))
        z_acc[ch] = y - _dot(yc.astype(BF16), xb)

    u = [jnp.concatenate([z_acc[(d, h)][0:HEAD_DIM] for h in range(GDN_HEADS)], axis=0).T for d in dirs]
    w = [jnp.concatenate([z_acc[(d, h)][HEAD_DIM:2 * HEAD_DIM] for h in range(GDN_HEADS)], axis=0).T
         for d in dirs]

    state_bd = (ri // HEAD_DIM) == (ci // HEAD_DIM)
    kt_t = [pre[d]["kt"].T for d in dirs]
    s = [s_refs[d % 2][d // 2] for d in dirs]
    outs = [[None] * N_SUB for _ in dirs]
    for i in range(N_SUB):
        for d in dirs:
            sub = i if d % 2 == 0 else N_SUB - 1 - i
            r0 = sub * GDN_CHUNK
            wq = jnp.concatenate([w[d][r0:r0 + GDN_CHUNK], pre[d]["qd"][r0:r0 + GDN_CHUNK]], axis=0)
            ws = _dot(wq.astype(BF16), s[d].astype(BF16))
            v_new = u[d][r0:r0 + GDN_CHUNK] - ws[0:GDN_CHUNK]
            o_sub = ws[GDN_CHUNK:2 * GDN_CHUNK]
            for h in range(GDN_HEADS):
                a_blk = attn[(d, h)][r0:r0 + GDN_CHUNK, r0:r0 + GDN_CHUNK].astype(BF16)
                o_sub = o_sub + _dot(a_blk, jnp.where(sub_masks[h], v_new, 0.0).astype(BF16))
            outs[d][sub] = o_sub
            upd = _dot(kt_t[d][:, r0:r0 + GDN_CHUNK].astype(BF16), v_new.astype(BF16))
            s[d] = s[d] * pre[d]["gtot_dec"][r0:r0 + 1, :] + jnp.where(state_bd, upd, 0.0)
    for d in dirs:
        s_refs[d % 2][d // 2] = s[d]
        o_refs[d % 2][d // 2] = jnp.concatenate(outs[d], axis=0)


def _gdn_consts():
    i = np.arange(TILE)
    same = (i[:, None] // GDN_CHUNK) == (i[None, :] // GDN_CHUNK)
    tri = np.stack([same & (i[None, :] <= i[:, None]), same & (i[None, :] >= i[:, None])])
    return jnp.asarray(same, BF16), jnp.asarray(tri, BF16)


def _gdn_call(qkvn, gb, consts, n_lat):
    b, tt, w = qkvn.shape
    nt = tt // TILE
    ones_chunk, tri = consts

    def f_idx(bi, c):
        return (bi, jnp.where(c == 0, n_lat, c - 1), 0)

    def b_idx(bi, c):
        return (bi, jnp.where(c == 0, n_lat, n_lat - c), 0)

    return pl.pallas_call(
        functools.partial(_gdn_kernel, n_lat),
        grid=(b // NB, nt),
        in_specs=[pl.BlockSpec((TILE, TILE), lambda bi, c: (0, 0)),
                  pl.BlockSpec((2, TILE, TILE), lambda bi, c: (0, 0, 0)),
                  pl.BlockSpec((NB, TILE, w), f_idx),
                  pl.BlockSpec((NB, TILE, LANES), f_idx),
                  pl.BlockSpec((NB, TILE, w), b_idx),
                  pl.BlockSpec((NB, TILE, LANES), b_idx)],
        out_specs=[pl.BlockSpec((NB, TILE, GDN_W), f_idx), pl.BlockSpec((NB, TILE, GDN_W), b_idx)],
        out_shape=[jax.ShapeDtypeStruct((b, tt, GDN_W), F32)] * 2,
        scratch_shapes=[pltpu.VMEM((NB, GDN_W, GDN_W), F32), pltpu.VMEM((NB, GDN_W, GDN_W), F32)],
        compiler_params=_cparams(2),
        name="gdn",
    )(ones_chunk, tri, qkvn, gb, qkvn, gb)


FF_CHUNK = 512


def _post_kernel(d_ff, final, n_lat_tiles, with_ctx, x_ref, *refs):
    if with_ctx:
        c_ref, refs = refs[0], refs[1:]
        is_lat = pl.program_id(1) < n_lat_tiles
        xs = [jnp.where(is_lat, x_ref[bb], c_ref[bb]) for bb in range(NB)]
    else:
        xs = [x_ref[bb] for bb in range(NB)]
    (rf_ref, rb_ref, rg_ref, ys_ref, gf_ref, gb_ref, gz_ref, g1_ref, sh_ref, sc_ref, g2_ref, nw_ref,
     fw_ref, rnw_ref, gnw_ref, ones_ref, wo_ref, w1_ref, w2_ref, o_ref) = refs
    ones_head = ones_ref[...]

    def rows(per_sample):
        return jnp.concatenate([per_sample(bb) for bb in range(NB)], axis=0)

    def head_out(f_ref, b_ref, gate_ref, w_ref):
        def one(bb):
            o = f_ref[bb] + b_ref[bb]
            ms = _head_sum(o * o, ones_head) * (1.0 / HEAD_DIM)
            return (o * lax.rsqrt(ms + EPS) * w_ref[...] * _silu(gate_ref[bb])).astype(BF16)
        return rows(one)

    mix = (_dot(head_out(rf_ref, rb_ref, rg_ref, rnw_ref), wo_ref[0:RET_W, :])
           + _dot(rows(lambda bb: ys_ref[bb].astype(BF16)), wo_ref[RET_W:RET_W + SWA_W, :])
           + _dot(head_out(gf_ref, gb_ref, gz_ref, gnw_ref), wo_ref[RET_W + SWA_W:, :]))
    x1 = [xs[bb] + g1_ref[bb] * mix[bb * TILE:(bb + 1) * TILE] for bb in range(NB)]

    def ffn_in(bb):
        y = x1[bb] * lax.rsqrt(jnp.mean(x1[bb] * x1[bb], axis=-1, keepdims=True) + EPS) * nw_ref[...]
        return (y * (1.0 + sc_ref[bb]) + sh_ref[bb]).astype(BF16)

    h = rows(ffn_in)
    acc = jnp.zeros((NB * TILE, x1[0].shape[1]), F32)
    for c0 in range(0, d_ff, FF_CHUNK):
        cw = min(FF_CHUNK, d_ff - c0)
        gate = _dot(h, w1_ref[:, c0:c0 + cw])
        up = _dot(h, w1_ref[:, d_ff + c0:d_ff + c0 + cw])
        acc = acc + _dot((_silu(gate) * up).astype(BF16), w2_ref[c0:c0 + cw, :])
    for bb in range(NB):
        x2 = x1[bb] + g2_ref[bb] * acc[bb * TILE:(bb + 1) * TILE]
        if final:
            x2 = x2 * lax.rsqrt(jnp.mean(x2 * x2, axis=-1, keepdims=True) + EPS) * fw_ref[...]
        o_ref[bb] = x2


def _post_call(x_lat, x_ctx, ctx_tile, o_rf, o_rb, ret, y_s, o_gf, o_gb, gz, mod, norm_w, final_w,
               ret_norm_w, gdn_norm_w, ones_head, wo_bf, w1_bf, w2_bf, layer, n_lat_tiles, ctx_row, final):
    b, _, d = x_lat.shape
    d_ff = w2_bf.shape[1]
    with_ctx = x_ctx is not None
    n_steps = n_lat_tiles + 1 if with_ctx else n_lat_tiles

    def mod_spec(k):
        return _mod_spec(d, k, n_lat_tiles if with_ctx else None, ctx_row, NB)

    def tile_spec(w):
        return pl.BlockSpec((NB, TILE, w), lambda bi, t: (bi, t, 0))

    full = lambda shape: pl.BlockSpec(shape, lambda bi, t: (0,) * len(shape))
    x_specs = _row_source_specs(d, n_lat_tiles, ctx_tile, NB) if with_ctx else [tile_spec(d)]
    x_args = (x_lat, x_ctx) if with_ctx else (x_lat,)
    return pl.pallas_call(
        functools.partial(_post_kernel, d_ff, final, n_lat_tiles, with_ctx),
        grid=(b // NB, n_steps),
        in_specs=x_specs + [
                  tile_spec(RET_W), tile_spec(RET_W),
                  pl.BlockSpec((NB, TILE, RET_W), lambda bi, t: (bi, t, 3)),
                  tile_spec(SWA_W), tile_spec(GDN_W), tile_spec(GDN_W), tile_spec(GDN_W),
                  mod_spec(2), mod_spec(3), mod_spec(4), mod_spec(5),
                  full((1, d)), full((1, d)), full((1, RET_W)), full((1, GDN_W)), full(ones_head.shape),
                  _layer_weight_spec(wo_bf, layer), _layer_weight_spec(w1_bf, layer),
                  _layer_weight_spec(w2_bf, layer)],
        out_specs=tile_spec(d),
        out_shape=jax.ShapeDtypeStruct((b, n_steps * TILE, d), F32),
        compiler_params=_cparams(2),
        name="out_proj_ffn",
    )(*x_args, o_rf, o_rb, ret, y_s, o_gf, o_gb, gz, mod, mod, mod, mod, norm_w.reshape(1, d),
      final_w.reshape(1, d), ret_norm_w.reshape(1, RET_W),
      jnp.tile(gdn_norm_w, GDN_HEADS).reshape(1, GDN_W), ones_head, wo_bf, w1_bf, w2_bf)


def _rope_tables(n_lat_rows, n_ctx_rows):
    n_grid_rows = n_lat_rows // GRID_W
    freqs = jnp.asarray(ROPE_BASE, F32) ** (-jnp.arange(N_FREQ, dtype=F32) / N_FREQ)
    ang_r = jnp.arange(n_grid_rows, dtype=F32)[:, None] * freqs[None, :]
    ang_c = jnp.arange(GRID_W, dtype=F32)[:, None] * freqs[None, :]

    def table(fr, fc, sign):
        r = jnp.broadcast_to(fr[:, None, :], (n_grid_rows, GRID_W, N_FREQ))
        c = jnp.broadcast_to(fc[None, :, :], (n_grid_rows, GRID_W, N_FREQ))
        head = jnp.concatenate([sign * r, r, sign * c, c], axis=-1).reshape(n_lat_rows, HEAD_DIM)
        return jnp.tile(head, (1, LANES // HEAD_DIM))

    cos = table(jnp.cos(ang_r), jnp.cos(ang_c), 1.0)
    sin = table(jnp.sin(ang_r), jnp.sin(ang_c), -1.0)
    cos = jnp.concatenate([cos, jnp.ones((n_ctx_rows, LANES), F32)], axis=0)
    sin = jnp.concatenate([sin, jnp.zeros((n_ctx_rows, LANES), F32)], axis=0)
    return cos, sin


def kernel(x, c, ctx, c_ctx, ada_w, ada_b, norm_mix_w, norm_ffn_w, w_in, ret_rate, ret_norm_w,
           swa_sinks, gdn_conv_w, gdn_a_log, gdn_dt_bias, gdn_norm_w, w_out, w_ffn_in, w_ffn_out,
           final_norm_w):
    b, l, d = x.shape
    lc = ctx.shape[1]
    depth = ada_w.shape[0]
    assert l % TILE == 0 and lc == TILE and d % LANES == 0
    assert b % NB == 0 and b % NB_ROWS == 0 and b % NB_WIDE == 0 and NB_ROWS % NB == 0 and b + NB_ROWS <= MOD_ROWS
    n_lat = l // TILE

    x_lat, x_ctx, ctx_tile = x, ctx, 0
    c_pad = jnp.zeros((MOD_ROWS, d), F32).at[:b].set(c).at[b:b + NB_ROWS].set(c_ctx)
    mod = _ada_call(c_pad, ada_w, ada_b)
    cos_t, sin_t = _rope_tables(l, lc)
    consts = _gdn_consts()
    i = np.arange(RET_W)
    ones_head = jnp.asarray((i[:, None] // HEAD_DIM) == (i[None, :] // HEAD_DIM), BF16)
    log_gamma = jnp.log1p(-jnp.exp2(-ret_rate.astype(F32)))

    w_in_bf = jnp.pad(w_in, ((0, 0), (0, 0), (0, IN_W_PAD - IN_W))).astype(BF16)
    wo_bf, w1_bf, w2_bf = w_out.astype(BF16), w_ffn_in.astype(BF16), w_ffn_out.astype(BF16)

    for layer in range(depth):
        last = layer == depth - 1
        mod_l = mod[layer].reshape(MOD_ROWS, 1, N_MOD * d)
        ret, swq, swkv, gqkv, gz, ab = _inproj_call(x_lat, x_ctx, ctx_tile, mod_l, norm_mix_w[layer],
                                                    w_in_bf, layer, cos_t, sin_t, n_lat, b)
        o_rf, o_rb = _ret_call(ret, log_gamma[layer], n_lat)
        n_lat_blk = l // Q_BLOCK
        n_q_blk = n_lat_blk if last else n_lat_blk + lc // Q_BLOCK
        y_s = _swa_call(swq, swkv, swa_sinks[layer], n_lat_blk, n_q_blk)
        qkvn, gb = _gdn_prep_call(gqkv, ab, gdn_conv_w[layer], gdn_a_log[layer], gdn_dt_bias[layer],
                                  ones_head, n_lat)
        o_gf, o_gb = _gdn_call(qkvn, gb, consts, n_lat)
        x_new = _post_call(x_lat, None if last else x_ctx, ctx_tile, o_rf, o_rb, ret, y_s, o_gf, o_gb, gz,
                           mod_l, norm_ffn_w[layer], final_norm_w, ret_norm_w[layer], gdn_norm_w[layer],
                           ones_head, wo_bf, w1_bf, w2_bf, layer, n_lat, b, last)
        x_lat, x_ctx, ctx_tile = x_new, x_new, n_lat
    return x_lat
```
